```python
import jax, jax.numpy as jnp
from jax import lax
import numpy as np

D_MODEL = 1024
BATCH = 8
SEQ = 8192
DEPTH = 2

EXPAND = 2
MIX_WIDTH = EXPAND * D_MODEL
GMLP_WIDTH = MIX_WIDTH // 2
GMLP_GROUPS = 4
GMLP_GROUP_DIM = GMLP_WIDTH // GMLP_GROUPS
GMLP_CHUNK = 128
ATTN_WIDTH = MIX_WIDTH // 2
HEAD_DIM = 128
N_Q_HEADS = ATTN_WIDTH // HEAD_DIM
N_KV_HEADS = 2
GQA_GROUP = N_Q_HEADS // N_KV_HEADS
KV_WIDTH = N_KV_HEADS * HEAD_DIM
WINDOW = 128
ATTN_BLOCK = 128
HGRN_WIDTH = MIX_WIDTH
HGRN_HEAD_DIM = 128
HGRN_HEADS = HGRN_WIDTH // HGRN_HEAD_DIM
HGRN_CHUNK = 64

EPS = 1e-6
N_EVEN = (DEPTH + 1) // 2
N_ODD = DEPTH // 2
EVEN_SPLITS = [GMLP_WIDTH, GMLP_WIDTH, GMLP_WIDTH, ATTN_WIDTH, KV_WIDTH, KV_WIDTH, ATTN_WIDTH]
ODD_SPLITS = [HGRN_WIDTH] * 5
IN_EVEN = sum(EVEN_SPLITS)
IN_ODD = sum(ODD_SPLITS)

kernel_name = "hybrid_gmlp_swa_hgrn2_encoder"

F32 = jnp.float32


def rms_norm(x, g):
    xf = x.astype(F32)
    y = xf * lax.rsqrt(jnp.mean(xf * xf, axis=-1, keepdims=True) + EPS)
    return (y * g.astype(F32)).astype(x.dtype)


def split_cols(t, sizes):
    idx = [int(i) for i in np.cumsum(sizes)[:-1]]
    return jnp.split(t, idx, axis=-1)


def alibi_slopes(n):
    return jnp.exp2(-8.0 * jnp.arange(1, n + 1, dtype=F32) / n)


def chunked_sgu(u, v, ln_g, ln_b, w_s, b_s):
    B, S, _ = v.shape
    vf = v.astype(F32)
    mu = jnp.mean(vf, axis=-1, keepdims=True)
    var = jnp.mean(jnp.square(vf - mu), axis=-1, keepdims=True)
    vn = ((vf - mu) * lax.rsqrt(var + EPS) * ln_g.astype(F32) + ln_b.astype(F32)).astype(v.dtype)
    vc = vn.reshape(B, S // GMLP_CHUNK, GMLP_CHUNK, GMLP_GROUPS, GMLP_GROUP_DIM)
    mixed = jnp.einsum('gts,bnsgc->bntgc', w_s, vc) + b_s.T[None, None, :, :, None]
    return u * mixed.reshape(B, S, GMLP_WIDTH)


def window_attention(q, k, v, sink):
    B, S = q.shape[0], q.shape[1]
    nb = S // ATTN_BLOCK
    qb = q.reshape(B, nb, ATTN_BLOCK, N_KV_HEADS, GQA_GROUP, HEAD_DIM)

    def band(t):
        tp = jnp.pad(t, ((0, 0), (ATTN_BLOCK, ATTN_BLOCK), (0, 0), (0, 0)))
        tb = tp.reshape(B, nb + 2, ATTN_BLOCK, N_KV_HEADS, HEAD_DIM)
        return jnp.concatenate([tb[:, :-2], tb[:, 1:-1], tb[:, 2:]], axis=2)

    kb, vb = band(k), band(v)
    scores = jnp.einsum('bnqhgd,bnshd->bnhgqs', qb, kb).astype(F32) * (HEAD_DIM ** -0.5)
    qi = jnp.arange(ATTN_BLOCK)
    kj = jnp.arange(3 * ATTN_BLOCK) - ATTN_BLOCK
    dist = jnp.abs(kj[None, :] - qi[:, None])
    kpos = (jnp.arange(nb) * ATTN_BLOCK)[:, None] + kj[None, :]
    valid = (dist[None] <= WINDOW) & (kpos[:, None, :] >= 0) & (kpos[:, None, :] < S)
    slopes = alibi_slopes(N_Q_HEADS).reshape(N_KV_HEADS, GQA_GROUP)
    scores = scores - slopes[:, :, None, None] * dist.astype(F32)
    scores = jnp.where(valid[None, :, None, None], scores, -jnp.inf)
    sink_l = sink.astype(F32).reshape(N_KV_HEADS, GQA_GROUP)[None, None, :, :, None, None]
    m = jnp.maximum(jnp.max(scores, axis=-1, keepdims=True), sink_l)
    p = jnp.exp(scores - m)
    p = p / (jnp.sum(p, axis=-1, keepdims=True) + jnp.exp(sink_l - m))
    out = jnp.einsum('bnhgqs,bnshd->bnqhgd', p.astype(v.dtype), vb)
    return out.reshape(B, S, N_Q_HEADS * HEAD_DIM)


def hgrn2_direction(q, f_logit, i, lb):
    B, S, H, D = q.shape
    nc = S // HGRN_CHUNK
    C = HGRN_CHUNK
    f = lb + (1.0 - lb) * jax.nn.sigmoid(f_logit.astype(F32))
    k = 1.0 - f
    g = jnp.log(f)

    def chunk(t):
        return t.reshape(B, nc, C, H, D)

    qc, kc, vc, gc = chunk(q.astype(F32)), chunk(k), chunk(i.astype(F32)), chunk(g)
    bcum = jnp.cumsum(gc, axis=2)
    blast = bcum[:, :, -1:]
    q_t = qc * jnp.exp(bcum)
    k_t = kc * jnp.exp(-bcum)
    k_end = kc * jnp.exp(blast - bcum)
    a = jnp.einsum('bnthd,bnshd->bnhts', q_t, k_t)
    a = jnp.where(jnp.tril(jnp.ones((C, C), dtype=bool)), a, 0.0)
    o_intra = jnp.einsum('bnhts,bnshv->bnthv', a, vc)

    def step(state, xs):
        qn, kn, vn, dn = xs
        o = jnp.einsum('bthd,bhdv->bthv', qn, state)
        state = state * dn[..., None] + jnp.einsum('bshd,bshv->bhdv', kn, vn)
        return state, o

    xs = (jnp.moveaxis(q_t, 1, 0), jnp.moveaxis(k_end, 1, 0), jnp.moveaxis(vc, 1, 0),
          jnp.moveaxis(jnp.exp(blast[:, :, 0]), 1, 0))
    s0 = jnp.zeros((B, H, D, D), dtype=F32)
    _, o_inter = lax.scan(step, s0, xs)
    o = o_intra + jnp.moveaxis(o_inter, 0, 1)
    return o.reshape(B, S, H, D)


def hgrn_lower_bound(gamma, layer):
    c = jnp.cumsum(jax.nn.softmax(gamma.astype(F32), axis=0), axis=0)
    return (c[layer] - c[0]).reshape(HGRN_HEADS, HGRN_HEAD_DIM)


def even_layer(x, norm_g, w_in, ln_g, ln_b, w_s, b_s, sink, w_out):
    B, S, _ = x.shape
    h = rms_norm(x, norm_g)
    proj = h @ w_in
    u_a, v_a, z_a, q_b, k_b, v_b, z_b = split_cols(proj, EVEN_SPLITS)
    a_out = chunked_sgu(u_a, v_a, ln_g, ln_b, w_s, b_s) * jax.nn.silu(z_a)
    b_out = window_attention(q_b.reshape(B, S, N_Q_HEADS, HEAD_DIM),
                             k_b.reshape(B, S, N_KV_HEADS, HEAD_DIM),
                             v_b.reshape(B, S, N_KV_HEADS, HEAD_DIM), sink) * jax.nn.silu(z_b)
    return jnp.concatenate([a_out, b_out], axis=-1) @ w_out


def odd_layer(x, norm_g, w_in, gamma_f, gamma_b, head_norm_g, w_out, layer):
    B, S, _ = x.shape
    h = rms_norm(x, norm_g)
    proj = h @ w_in
    q, f_f, f_b, i, z = split_cols(proj, ODD_SPLITS)
    shp = (B, S, HGRN_HEADS, HGRN_HEAD_DIM)
    q = jax.nn.silu(q).reshape(shp)
    i = i.reshape(shp)
    o_f = hgrn2_direction(q, f_f.reshape(shp), i, hgrn_lower_bound(gamma_f, layer))
    o_b = jnp.flip(hgrn2_direction(jnp.flip(q, 1), jnp.flip(f_b.reshape(shp), 1), jnp.flip(i, 1),
                                   hgrn_lower_bound(gamma_b, layer)), 1)
    o = o_f + o_b
    o = o * lax.rsqrt(jnp.mean(o * o, axis=-1, keepdims=True) + EPS)
    o = (o.reshape(B, S, HGRN_WIDTH) * head_norm_g.astype(F32)).astype(x.dtype)
    return (o * jax.nn.silu(z)) @ w_out


def setup_inputs(seed: int = 0) -> dict:
    key = jax.random.key(seed)
    ks = jax.random.split(key, 16)
    nrm = jax.random.normal
    return {
        "x": nrm(ks[0], (BATCH, SEQ, D_MODEL), F32),
        "norm_g_even": 1.0 + 0.02 * nrm(ks[1], (N_EVEN, D_MODEL), F32),
        "w_in_even": nrm(ks[2], (N_EVEN, D_MODEL, IN_EVEN), F32) * D_MODEL ** -0.5,
        "gmlp_ln_g": 1.0 + 0.02 * nrm(ks[3], (N_EVEN, GMLP_WIDTH), F32),
        "gmlp_ln_b": 0.02 * nrm(ks[4], (N_EVEN, GMLP_WIDTH), F32),
        "gmlp_w_s": nrm(ks[5], (N_EVEN, GMLP_GROUPS, GMLP_CHUNK, GMLP_CHUNK), F32) * GMLP_CHUNK ** -0.5,
        "gmlp_b_s": 1.0 + 0.02 * nrm(ks[6], (N_EVEN, GMLP_GROUPS, GMLP_CHUNK), F32),
        "attn_sink": 0.5 * nrm(ks[7], (N_EVEN, N_Q_HEADS), F32),
        "w_out_even": nrm(ks[8], (N_EVEN, GMLP_WIDTH + ATTN_WIDTH, D_MODEL), F32) * (GMLP_WIDTH + ATTN_WIDTH) ** -0.5,
        "norm_g_odd": 1.0 + 0.02 * nrm(ks[9], (N_ODD, D_MODEL), F32),
        "w_in_odd": nrm(ks[10], (N_ODD, D_MODEL, IN_ODD), F32) * D_MODEL ** -0.5,
        "hgrn_gamma_fwd": 1.0 + 0.1 * nrm(ks[11], (DEPTH, HGRN_WIDTH), F32),
        "hgrn_gamma_bwd": 1.0 + 0.1 * nrm(ks[12], (DEPTH, HGRN_WIDTH), F32),
        "hgrn_head_norm_g": 1.0 + 0.02 * nrm(ks[13], (N_ODD, HGRN_WIDTH), F32),
        "w_out_odd": nrm(ks[14], (N_ODD, HGRN_WIDTH, D_MODEL), F32) * HGRN_WIDTH ** -0.5,
        "final_norm_g": 1.0 + 0.02 * nrm(ks[15], (D_MODEL,), F32),
    }


def reference(x, norm_g_even, w_in_even, gmlp_ln_g, gmlp_ln_b, gmlp_w_s, gmlp_b_s, attn_sink,
              w_out_even, norm_g_odd, w_in_odd, hgrn_gamma_fwd, hgrn_gamma_bwd, hgrn_head_norm_g,
              w_out_odd, final_norm_g):
    for layer in range(DEPTH):
        j = layer // 2
        if layer % 2 == 0:
            x = x + even_layer(x, norm_g_even[j], w_in_even[j], gmlp_ln_g[j], gmlp_ln_b[j],
                               gmlp_w_s[j], gmlp_b_s[j], attn_sink[j], w_out_even[j])
        else:
            x = x + odd_layer(x, norm_g_odd[j], w_in_odd[j], hgrn_gamma_fwd, hgrn_gamma_bwd,
                              hgrn_head_norm_g[j], w_out_odd[j], layer)
    return rms_norm(x, final_norm_g)
```

```python
import functools

import jax
import jax.numpy as jnp
import numpy as np
from jax import lax
from jax.experimental import pallas as pl
from jax.experimental.pallas import tpu as pltpu

F32 = jnp.float32
BF16 = jnp.bfloat16

D_MODEL = 1024
DEPTH = 2
MIX_WIDTH = 2 * D_MODEL
GMLP_WIDTH = MIX_WIDTH // 2
GMLP_GROUPS = 4
GMLP_GROUP_DIM = GMLP_WIDTH // GMLP_GROUPS
GMLP_CHUNK = 128
ATTN_WIDTH = MIX_WIDTH // 2
HEAD_DIM = 128
N_Q_HEADS = ATTN_WIDTH // HEAD_DIM
N_KV_HEADS = 2
GQA_GROUP = N_Q_HEADS // N_KV_HEADS
KV_WIDTH = N_KV_HEADS * HEAD_DIM
WINDOW = 128
ATTN_BLOCK = 128
HGRN_WIDTH = MIX_WIDTH
HGRN_HEAD_DIM = 128
HGRN_HEADS = HGRN_WIDTH // HGRN_HEAD_DIM
HGRN_CHUNK = 64
EPS = 1e-6
IN_EVEN = 3 * GMLP_WIDTH + ATTN_WIDTH + 2 * KV_WIDTH + ATTN_WIDTH

_OFF_UVZ = 0
_OFF_Q = 3 * GMLP_WIDTH
_OFF_KV = _OFF_Q + ATTN_WIDTH
_OFF_ZB = _OFF_KV + 2 * KV_WIDTH

V7X_VMEM_LIMIT_BYTES = 58 * 1024 * 1024

EVEN_TILE = 512
ODD_TILE = 256

_NT = (((1,), (1,)), ((), ()))
_TN = (((0,), (0,)), ((), ()))


def _const_spec(shape):
    nd = len(shape)
    return pl.BlockSpec(shape, lambda *_: (0,) * nd, pipeline_mode=pl.Buffered(1))


def _rms_rows(x, g):
    return x * lax.rsqrt(jnp.mean(x * x, axis=-1, keepdims=True) + EPS) * g


def _silu(z):
    return z * jax.nn.sigmoid(z)


def _even_kernel(x_ref, xp_ref, xn_ref, ng_ref, win_ref, lng_ref, lnb_ref, ws_ref, bs_ref,
                 ab_ref, sink_ref, wout_ref, o_ref,
                 hn_scr, uvz_scr, q_scr, kv_scr, zb_scr, y_scr, *, tile, seq):
    t = pl.program_id(1)
    ng = ng_ref[...]
    hn_scr[0:ATTN_BLOCK, :] = _rms_rows(xp_ref[0], ng).astype(BF16)
    hn_scr[ATTN_BLOCK:ATTN_BLOCK + tile, :] = _rms_rows(x_ref[0], ng).astype(BF16)
    hn_scr[ATTN_BLOCK + tile:, :] = _rms_rows(xn_ref[0], ng).astype(BF16)

    hn_cur = hn_scr[ATTN_BLOCK:ATTN_BLOCK + tile, :]
    uvz_scr[...] = jnp.dot(hn_cur, win_ref[:, _OFF_UVZ:_OFF_Q], preferred_element_type=F32)
    q_scr[...] = (jnp.dot(hn_cur, win_ref[:, _OFF_Q:_OFF_KV], preferred_element_type=F32)
                  * (HEAD_DIM ** -0.5)).astype(BF16)
    kv_scr[...] = jnp.dot(hn_scr[...], win_ref[:, _OFF_KV:_OFF_ZB], preferred_element_type=F32).astype(BF16)
    zb_scr[...] = jnp.dot(hn_cur, win_ref[:, _OFF_ZB:], preferred_element_type=F32)

    lng = lng_ref[...]
    lnb = lnb_ref[...]
    kj = lax.broadcasted_iota(jnp.int32, (1, 3 * ATTN_BLOCK), 1)

    def block_body(j, carry):
        r0 = pl.multiple_of(j * ATTN_BLOCK, ATTN_BLOCK)
        rows = pl.ds(r0, ATTN_BLOCK)
        v = uvz_scr[rows, GMLP_WIDTH:2 * GMLP_WIDTH]
        mu = jnp.mean(v, axis=-1, keepdims=True)
        vc = v - mu
        var = jnp.mean(vc * vc, axis=-1, keepdims=True)
        vn = (vc * lax.rsqrt(var + EPS) * lng + lnb).astype(BF16)
        mixed = jnp.concatenate(
            [jnp.dot(ws_ref[g], vn[:, g * GMLP_GROUP_DIM:(g + 1) * GMLP_GROUP_DIM], preferred_element_type=F32)
             for g in range(GMLP_GROUPS)], axis=1) + bs_ref[...]
        u = uvz_scr[rows, 0:GMLP_WIDTH]
        za = uvz_scr[rows, 2 * GMLP_WIDTH:3 * GMLP_WIDTH]
        y_scr[rows, 0:GMLP_WIDTH] = (u * mixed * _silu(za)).astype(BF16)

        kpos = t * tile + r0 - ATTN_BLOCK + kj
        in_seq = (kpos >= 0) & (kpos < seq)
        win = pl.ds(r0, 3 * ATTN_BLOCK)
        for hk in range(N_KV_HEADS):
            qs = jnp.concatenate(
                [q_scr[rows, (hk * GQA_GROUP + g) * HEAD_DIM:(hk * GQA_GROUP + g + 1) * HEAD_DIM]
                 for g in range(GQA_GROUP)], axis=0)
            kw = kv_scr[win, hk * HEAD_DIM:(hk + 1) * HEAD_DIM]
            vw = kv_scr[win, KV_WIDTH + hk * HEAD_DIM:KV_WIDTH + (hk + 1) * HEAD_DIM]
            s = lax.dot_general(qs, kw, _NT, preferred_element_type=F32) + ab_ref[hk]
            s = jnp.where(in_seq, s, -jnp.inf)
            sink = sink_ref[hk]
            m = jnp.maximum(jnp.max(s, axis=-1, keepdims=True), sink)
            p = jnp.exp(s - m)
            den = jnp.sum(p, axis=-1, keepdims=True) + jnp.exp(sink - m)
            o = jnp.dot(p.astype(BF16), vw, preferred_element_type=F32) / den
            for g in range(GQA_GROUP):
                c0 = (hk * GQA_GROUP + g) * HEAD_DIM
                zb = zb_scr[rows, c0:c0 + HEAD_DIM]
                y_scr[rows, GMLP_WIDTH + c0:GMLP_WIDTH + c0 + HEAD_DIM] = (
                    o[g * ATTN_BLOCK:(g + 1) * ATTN_BLOCK] * _silu(zb)).astype(BF16)
        return carry

    lax.fori_loop(0, tile // ATTN_BLOCK, block_body, 0)
    o_ref[0] = x_ref[0] + jnp.dot(y_scr[...], wout_ref[...], preferred_element_type=F32)


def _even_layer(x, norm_g, w_in, ln_g, ln_b, w_s, b_s, sink, w_out):
    B, S, D = x.shape
    tile = min(EVEN_TILE, S)
    assert S % tile == 0 and tile % ATTN_BLOCK == 0
    nb = S // ATTN_BLOCK
    tpb = tile // ATTN_BLOCK

    qi = np.arange(ATTN_BLOCK)
    kj = np.arange(3 * ATTN_BLOCK) - ATTN_BLOCK
    dist = np.abs(kj[None, :] - qi[:, None]).astype(np.float32)
    slopes = np.exp2(-8.0 * np.arange(1, N_Q_HEADS + 1, dtype=np.float32) / N_Q_HEADS)
    ab = np.where(dist[None] <= WINDOW, -slopes[:, None, None] * dist[None], -np.inf).astype(np.float32)
    ab = jnp.asarray(ab.reshape(N_KV_HEADS, GQA_GROUP * ATTN_BLOCK, 3 * ATTN_BLOCK))
    sink_col = jnp.broadcast_to(sink.astype(F32).reshape(N_KV_HEADS, GQA_GROUP, 1, 1),
                                (N_KV_HEADS, GQA_GROUP, ATTN_BLOCK, 1)).reshape(N_KV_HEADS, GQA_GROUP * ATTN_BLOCK, 1)
    bs_full = jnp.repeat(b_s.astype(F32).T, GMLP_GROUP_DIM, axis=1)

    kern = functools.partial(_even_kernel, tile=tile, seq=S)
    return pl.pallas_call(
        kern,
        grid=(B, S // tile),
        in_specs=[
            pl.BlockSpec((1, tile, D), lambda b, t: (b, t, 0)),
            pl.BlockSpec((1, ATTN_BLOCK, D), lambda b, t: (b, jnp.maximum(t * tpb - 1, 0), 0)),
            pl.BlockSpec((1, ATTN_BLOCK, D), lambda b, t: (b, jnp.minimum((t + 1) * tpb, nb - 1), 0)),
            _const_spec((1, D)),
            _const_spec((D, IN_EVEN)),
            _const_spec((1, GMLP_WIDTH)),
            _const_spec((1, GMLP_WIDTH)),
            _const_spec((GMLP_GROUPS, GMLP_CHUNK, GMLP_CHUNK)),
            _const_spec((GMLP_CHUNK, GMLP_WIDTH)),
            _const_spec((N_KV_HEADS, GQA_GROUP * ATTN_BLOCK, 3 * ATTN_BLOCK)),
            _const_spec((N_KV_HEADS, GQA_GROUP * ATTN_BLOCK, 1)),
            _const_spec((GMLP_WIDTH + ATTN_WIDTH, D)),
        ],
        out_specs=pl.BlockSpec((1, tile, D), lambda b, t: (b, t, 0)),
        out_shape=jax.ShapeDtypeStruct((B, S, D), x.dtype),
        scratch_shapes=[
            pltpu.VMEM((tile + 2 * ATTN_BLOCK, D), BF16),
            pltpu.VMEM((tile, 3 * GMLP_WIDTH), F32),
            pltpu.VMEM((tile, ATTN_WIDTH), BF16),
            pltpu.VMEM((tile + 2 * ATTN_BLOCK, 2 * KV_WIDTH), BF16),
            pltpu.VMEM((tile, ATTN_WIDTH), F32),
            pltpu.VMEM((tile, GMLP_WIDTH + ATTN_WIDTH), BF16),
        ],
        compiler_params=pltpu.CompilerParams(
            dimension_semantics=("arbitrary", "arbitrary"),
            vmem_limit_bytes=V7X_VMEM_LIMIT_BYTES),
        name="even_layer",
    )(x, x, x, norm_g.reshape(1, D).astype(F32), w_in.astype(BF16),
      ln_g.reshape(1, -1).astype(F32), ln_b.reshape(1, -1).astype(F32), w_s.astype(BF16), bs_full,
      ab, sink_col, w_out.astype(BF16))


def _hgrn_lower_bound(gamma, layer):
    rows = [gamma[l:l + 1, :] for l in range(DEPTH)]
    mx = functools.reduce(jnp.maximum, rows)
    es = [jnp.exp(r - mx) for r in rows]
    tot = functools.reduce(lambda a, b: a + b, es)
    acc = jnp.zeros_like(tot)
    for l in range(1, layer + 1):
        acc = acc + es[l]
    return acc / tot


def _hgrn_pass(proj_ref, lb_ref, state_ref, out_ref, *, tile, reverse, off_q, off_f, off_i):
    C = HGRN_CHUNK
    nchunks = tile // C
    ti = lax.broadcasted_iota(jnp.int32, (C, C), 0)
    si = lax.broadcasted_iota(jnp.int32, (C, C), 1)
    if reverse:
        cmask = si >= ti
    else:
        cmask = si <= ti
    scan3 = jnp.concatenate([cmask.astype(BF16)] * 3, axis=1)

    def chunk_body(ci, carry):
        c = (nchunks - 1 - ci) if reverse else ci
        rows = pl.ds(pl.multiple_of(c * C, C), C)

        def head_body(h, carry2):
            hc = pl.multiple_of(h * HGRN_HEAD_DIM, HGRN_HEAD_DIM)
            q = _silu(proj_ref[rows, pl.ds(off_q + hc, HGRN_HEAD_DIM)])
            fl = proj_ref[rows, pl.ds(off_f + hc, HGRN_HEAD_DIM)]
            v = proj_ref[rows, pl.ds(off_i + hc, HGRN_HEAD_DIM)].astype(BF16)
            lb = lb_ref[:, pl.ds(hc, HGRN_HEAD_DIM)]
            f = lb + (1.0 - lb) * jax.nn.sigmoid(fl)
            k = 1.0 - f
            g = jnp.log(f)
            g1 = g.astype(BF16)
            r1 = g - g1.astype(F32)
            g2 = r1.astype(BF16)
            g3 = (r1 - g2.astype(F32)).astype(BF16)
            bc = jnp.dot(scan3, jnp.concatenate([g1, g2, g3], axis=0), preferred_element_type=F32)
            blast = bc[0:1, :] if reverse else bc[C - 1:C, :]
            qt = (q * jnp.exp(bc)).astype(BF16)
            kt = (k * jnp.exp(-bc)).astype(BF16)
            kend = (k * jnp.exp(blast - bc)).astype(BF16)
            a = lax.dot_general(qt, kt, _NT, preferred_element_type=F32)
            a = jnp.where(cmask, a, 0.0).astype(BF16)
            st = state_ref[h]
            o = (jnp.dot(a, v, preferred_element_type=F32)
                 + lax.dot_general(qt, st.astype(BF16), _NT, preferred_element_type=F32))
            state_ref[h] = st * jnp.exp(blast) + lax.dot_general(v, kend, _TN, preferred_element_type=F32)
            out_ref[rows, pl.ds(hc, HGRN_HEAD_DIM)] = o
            return carry2

        lax.fori_loop(0, HGRN_HEADS, head_body, 0)
        return carry

    lax.fori_loop(0, nchunks, chunk_body, 0)


def _odd_bwd_kernel(x_ref, ng_ref, w_ref, gamma_ref, ob_ref, proj_scr, lb_scr, state_scr, *, tile, layer):
    @pl.when(pl.program_id(1) == 0)
    def _():
        state_scr[...] = jnp.zeros_like(state_scr)

    lb_scr[...] = _hgrn_lower_bound(gamma_ref[...], layer)
    hn = _rms_rows(x_ref[0], ng_ref[...]).astype(BF16)
    proj_scr[...] = jnp.dot(hn, w_ref[...], preferred_element_type=F32)
    _hgrn_pass(proj_scr, lb_scr, state_scr, ob_ref.at[0], tile=tile, reverse=True,
               off_q=0, off_f=HGRN_WIDTH, off_i=2 * HGRN_WIDTH)


def _odd_fwd_kernel(x_ref, ob_ref, ng_ref, w_ref, gamma_ref, hng_ref, wout_ref, fg_ref, o_ref,
                    proj_scr, lb_scr, state_scr, of_scr, y_scr, *, tile, layer):
    @pl.when(pl.program_id(1) == 0)
    def _():
        state_scr[...] = jnp.zeros_like(state_scr)

    lb_scr[...] = _hgrn_lower_bound(gamma_ref[...], layer)
    x = x_ref[0]
    hn = _rms_rows(x, ng_ref[...]).astype(BF16)
    proj_scr[...] = jnp.dot(hn, w_ref[...], preferred_element_type=F32)
    _hgrn_pass(proj_scr, lb_scr, state_scr, of_scr, tile=tile, reverse=False,
               off_q=0, off_f=HGRN_WIDTH, off_i=2 * HGRN_WIDTH)
    for h in range(HGRN_HEADS):
        cs = slice(h * HGRN_HEAD_DIM, (h + 1) * HGRN_HEAD_DIM)
        o = of_scr[:, cs] + ob_ref[0, :, cs]
        o = o * lax.rsqrt(jnp.mean(o * o, axis=-1, keepdims=True) + EPS) * hng_ref[:, cs]
        z = proj_scr[:, 3 * HGRN_WIDTH + h * HGRN_HEAD_DIM:3 * HGRN_WIDTH + (h + 1) * HGRN_HEAD_DIM]
        y_scr[:, cs] = (o * _silu(z)).astype(BF16)
    x2 = x + jnp.dot(y_scr[...], wout_ref[...], preferred_element_type=F32)
    o_ref[0] = _rms_rows(x2, fg_ref[...])


def _odd_layer_and_final_norm(x, norm_g, w_in, gamma_f, gamma_b, head_norm_g, w_out, final_g, layer):
    B, S, D = x.shape
    tile = min(ODD_TILE, S)
    assert S % tile == 0 and tile % HGRN_CHUNK == 0
    nt = S // tile
    W = HGRN_WIDTH
    w_bf = w_in.astype(BF16)
    wq, wff, wfb, wi, wz = (w_bf[:, n * W:(n + 1) * W] for n in range(5))
    w_bwd = jnp.concatenate([wq, wfb, wi], axis=1)
    w_fwd = jnp.concatenate([wq, wff, wi, wz], axis=1)
    ng = norm_g.reshape(1, D).astype(F32)
    params = pltpu.CompilerParams(dimension_semantics=("arbitrary", "arbitrary"),
                                  vmem_limit_bytes=V7X_VMEM_LIMIT_BYTES)

    o_b = pl.pallas_call(
        functools.partial(_odd_bwd_kernel, tile=tile, layer=layer),
        grid=(B, nt),
        in_specs=[
            pl.BlockSpec((1, tile, D), lambda b, t: (b, nt - 1 - t, 0)),
            _const_spec((1, D)),
            _const_spec((D, 3 * W)),
            _const_spec((DEPTH, W)),
        ],
        out_specs=pl.BlockSpec((1, tile, W), lambda b, t: (b, nt - 1 - t, 0)),
        out_shape=jax.ShapeDtypeStruct((B, S, W), F32),
        scratch_shapes=[
            pltpu.VMEM((tile, 3 * W), F32),
            pltpu.VMEM((1, W), F32),
            pltpu.VMEM((HGRN_HEADS, HGRN_HEAD_DIM, HGRN_HEAD_DIM), F32),
        ],
        compiler_params=params,
        name="odd_layer_bwd_pass",
    )(x, ng, w_bwd, gamma_b.astype(F32))

    return pl.pallas_call(
        functools.partial(_odd_fwd_kernel, tile=tile, layer=layer),
        grid=(B, nt),
        in_specs=[
            pl.BlockSpec((1, tile, D), lambda b, t: (b, t, 0)),
            pl.BlockSpec((1, tile, W), lambda b, t: (b, t, 0)),
            _const_spec((1, D)),
            _const_spec((D, 4 * W)),
            _const_spec((DEPTH, W)),
            _const_spec((1, W)),
            _const_spec((W, D)),
            _const_spec((1, D)),
        ],
        out_specs=pl.BlockSpec((1, tile, D), lambda b, t: (b, t, 0)),
        out_shape=jax.ShapeDtypeStruct((B, S, D), x.dtype),
        scratch_shapes=[
            pltpu.VMEM((tile, 4 * W), F32),
            pltpu.VMEM((1, W), F32),
            pltpu.VMEM((HGRN_HEADS, HGRN_HEAD_DIM, HGRN_HEAD_DIM), F32),
            pltpu.VMEM((tile, W), F32),
            pltpu.VMEM((tile, W), BF16),
        ],
        compiler_params=params,
        name="odd_layer_fwd_pass",
    )(x, o_b, ng, w_fwd, gamma_f.astype(F32), head_norm_g.reshape(1, W).astype(F32),
      w_out.astype(BF16), final_g.reshape(1, D).astype(F32))


def kernel(x, norm_g_even, w_in_even, gmlp_ln_g, gmlp_ln_b, gmlp_w_s, gmlp_b_s, attn_sink, w_out_even, norm_g_odd, w_in_odd, hgrn_gamma_fwd, hgrn_gamma_bwd, hgrn_head_norm_g, w_out_odd, final_norm_g):
    assert DEPTH == 2
    x = _even_layer(x, norm_g_even[0], w_in_even[0], gmlp_ln_g[0], gmlp_ln_b[0], gmlp_w_s[0], gmlp_b_s[0],
                    attn_sink[0], w_out_even[0])
    return _odd_layer_and_final_norm(x, norm_g_odd[0], w_in_odd[0], hgrn_gamma_fwd, hgrn_gamma_bwd,
                                     hgrn_head_norm_g[0], w_out_odd[0], final_norm_g, layer=1)
```

```python
import functools

import jax
import jax.numpy as jnp
import numpy as np
from jax import lax
from jax.experimental import pallas as pl
from jax.experimental.pallas import tpu as pltpu

F32 = jnp.float32
BF16 = jnp.bfloat16

D_MODEL = 1024
DEPTH = 2
MIX_WIDTH = 2 * D_MODEL
GMLP_WIDTH = MIX_WIDTH // 2
GMLP_GROUPS = 4
GMLP_GROUP_DIM = GMLP_WIDTH // GMLP_GROUPS
GMLP_CHUNK = 128
ATTN_WIDTH = MIX_WIDTH // 2
HEAD_DIM = 128
N_Q_HEADS = ATTN_WIDTH // HEAD_DIM
N_KV_HEADS = 2
GQA_GROUP = N_Q_HEADS // N_KV_HEADS
KV_WIDTH = N_KV_HEADS * HEAD_DIM
WINDOW = 128
ATTN_BLOCK = 128
HGRN_WIDTH = MIX_WIDTH
HGRN_HEAD_DIM = 128
HGRN_HEADS = HGRN_WIDTH // HGRN_HEAD_DIM
HGRN_CHUNK = 64
EPS = 1e-6
IN_EVEN = 3 * GMLP_WIDTH + ATTN_WIDTH + 2 * KV_WIDTH + ATTN_WIDTH

_OFF_UVZ = 0
_OFF_Q = 3 * GMLP_WIDTH
_OFF_KV = _OFF_Q + ATTN_WIDTH
_OFF_ZB = _OFF_KV + 2 * KV_WIDTH

V7X_VMEM_LIMIT_BYTES = 58 * 1024 * 1024

EVEN_TILE = 512
ODD_TILE = 256

_NT = (((1,), (1,)), ((), ()))
_TN = (((0,), (0,)), ((), ()))


def _const_spec(shape):
    nd = len(shape)
    return pl.BlockSpec(shape, lambda *_: (0,) * nd, pipeline_mode=pl.Buffered(1))


def _rms_rows(x, g):
    return x * lax.rsqrt(jnp.mean(x * x, axis=-1, keepdims=True) + EPS) * g


def _silu(z):
    return z * jax.nn.sigmoid(z)


def _even_kernel(x_ref, xp_ref, xn_ref, ng_ref, win_ref, lng_ref, lnb_ref, ws_ref, bs_ref,
                 ab_ref, sink_ref, wout_ref, o_ref,
                 hn_scr, uvz_scr, q_scr, kv_scr, zb_scr, y_scr, *, tile, seq):
    t = pl.program_id(1)
    ng = ng_ref[...]
    hn_scr[0:ATTN_BLOCK, :] = _rms_rows(xp_ref[0], ng).astype(BF16)
    hn_scr[ATTN_BLOCK:ATTN_BLOCK + tile, :] = _rms_rows(x_ref[0], ng).astype(BF16)
    hn_scr[ATTN_BLOCK + tile:, :] = _rms_rows(xn_ref[0], ng).astype(BF16)

    hn_cur = hn_scr[ATTN_BLOCK:ATTN_BLOCK + tile, :]
    uvz_scr[...] = jnp.dot(hn_cur, win_ref[:, _OFF_UVZ:_OFF_Q], preferred_element_type=F32)
    q_scr[...] = (jnp.dot(hn_cur, win_ref[:, _OFF_Q:_OFF_KV], preferred_element_type=F32)
                  * (HEAD_DIM ** -0.5)).astype(BF16)
    kv_scr[...] = jnp.dot(hn_scr[...], win_ref[:, _OFF_KV:_OFF_ZB], preferred_element_type=F32).astype(BF16)
    zb_scr[...] = jnp.dot(hn_cur, win_ref[:, _OFF_ZB:], preferred_element_type=F32)

    lng = lng_ref[...]
    lnb = lnb_ref[...]
    kj = lax.broadcasted_iota(jnp.int32, (1, 3 * ATTN_BLOCK), 1)

    def block_body(j, carry):
        r0 = pl.multiple_of(j * ATTN_BLOCK, ATTN_BLOCK)
        rows = pl.ds(r0, ATTN_BLOCK)
        v = uvz_scr[rows, GMLP_WIDTH:2 * GMLP_WIDTH]
        mu = jnp.mean(v, axis=-1, keepdims=True)
        vc = v - mu
        var = jnp.mean(vc * vc, axis=-1, keepdims=True)
        vn = (vc * lax.rsqrt(var + EPS) * lng + lnb).astype(BF16)
        mixed = jnp.concatenate(
            [jnp.dot(ws_ref[g], vn[:, g * GMLP_GROUP_DIM:(g + 1) * GMLP_GROUP_DIM], preferred_element_type=F32)
             for g in range(GMLP_GROUPS)], axis=1) + bs_ref[...]
        u = uvz_scr[rows, 0:GMLP_WIDTH]
        za = uvz_scr[rows, 2 * GMLP_WIDTH:3 * GMLP_WIDTH]
        y_scr[rows, 0:GMLP_WIDTH] = (u * mixed * _silu(za)).astype(BF16)

        kpos = t * tile + r0 - ATTN_BLOCK + kj
        in_seq = (kpos >= 0) & (kpos < seq)
        win = pl.ds(r0, 3 * ATTN_BLOCK)
        for hk in range(N_KV_HEADS):
            qs = jnp.concatenate(
                [q_scr[rows, (hk * GQA_GROUP + g) * HEAD_DIM:(hk * GQA_GROUP + g + 1) * HEAD_DIM]
                 for g in range(GQA_GROUP)], axis=0)
            kw = kv_scr[win, hk * HEAD_DIM:(hk + 1) * HEAD_DIM]
            vw = kv_scr[win, KV_WIDTH + hk * HEAD_DIM:KV_WIDTH + (hk + 1) * HEAD_DIM]
            s = lax.dot_general(qs, kw, _NT, preferred_element_type=F32) + ab_ref[hk]
            s = jnp.where(in_seq, s, -jnp.inf)
            sink = sink_ref[hk]
            m = jnp.maximum(jnp.max(s, axis=-1, keepdims=True), sink)
            p = jnp.exp(s - m)
            den = jnp.sum(p, axis=-1, keepdims=True) + jnp.exp(sink - m)
            o = jnp.dot(p.astype(BF16), vw, preferred_element_type=F32) / den
            for g in range(GQA_GROUP):
                c0 = (hk * GQA_GROUP + g) * HEAD_DIM
                zb = zb_scr[rows, c0:c0 + HEAD_DIM]
                y_scr[rows, GMLP_WIDTH + c0:GMLP_WIDTH + c0 + HEAD_DIM] = (
                    o[g * ATTN_BLOCK:(g + 1) * ATTN_BLOCK] * _silu(zb)).astype(BF16)
        return carry

    lax.fori_loop(0, tile // ATTN_BLOCK, block_body, 0)
    o_ref[0] = x_ref[0] + jnp.dot(y_scr[...], wout_ref[...], preferred_element_type=F32)


def _even_layer(x, norm_g, w_in, ln_g, ln_b, w_s, b_s, sink, w_out):
    B, S, D = x.shape
    tile = min(EVEN_TILE, S)
    assert S % tile == 0 and tile % ATTN_BLOCK == 0
    nb = S // ATTN_BLOCK
    tpb = tile // ATTN_BLOCK

    qi = np.arange(ATTN_BLOCK)
    kj = np.arange(3 * ATTN_BLOCK) - ATTN_BLOCK
    dist = np.abs(kj[None, :] - qi[:, None]).astype(np.float32)
    slopes = np.exp2(-8.0 * np.arange(1, N_Q_HEADS + 1, dtype=np.float32) / N_Q_HEADS)
    ab = np.where(dist[None] <= WINDOW, -slopes[:, None, None] * dist[None], -np.inf).astype(np.float32)
    ab = jnp.asarray(ab.reshape(N_KV_HEADS, GQA_GROUP * ATTN_BLOCK, 3 * ATTN_BLOCK))
    sink_col = jnp.broadcast_to(sink.astype(F32).reshape(N_KV_HEADS, GQA_GROUP, 1, 1),
                                (N_KV_HEADS, GQA_GROUP, ATTN_BLOCK, 1)).reshape(N_KV_HEADS, GQA_GROUP * ATTN_BLOCK, 1)
    bs_full = jnp.repeat(b_s.astype(F32).T, GMLP_GROUP_DIM, axis=1)

    kern = functools.partial(_even_kernel, tile=tile, seq=S)
    return pl.pallas_call(
        kern,
        grid=(B, S // tile),
        in_specs=[
            pl.BlockSpec((1, tile, D), lambda b, t: (b, t, 0)),
            pl.BlockSpec((1, ATTN_BLOCK, D), lambda b, t: (b, jnp.maximum(t * tpb - 1, 0), 0)),
            pl.BlockSpec((1, ATTN_BLOCK, D), lambda b, t: (b, jnp.minimum((t + 1) * tpb, nb - 1), 0)),
            _const_spec((1, D)),
            _const_spec((D, IN_EVEN)),
            _const_spec((1, GMLP_WIDTH)),
            _const_spec((1, GMLP_WIDTH)),
            _const_spec((GMLP_GROUPS, GMLP_CHUNK, GMLP_CHUNK)),
            _const_spec((GMLP_CHUNK, GMLP_WIDTH)),
            _const_spec((N_KV_HEADS, GQA_GROUP * ATTN_BLOCK, 3 * ATTN_BLOCK)),
            _const_spec((N_KV_HEADS, GQA_GROUP * ATTN_BLOCK, 1)),
            _const_spec((GMLP_WIDTH + ATTN_WIDTH, D)),
        ],
        out_specs=pl.BlockSpec((1, tile, D), lambda b, t: (b, t, 0)),
        out_shape=jax.ShapeDtypeStruct((B, S, D), x.dtype),
        scratch_shapes=[
            pltpu.VMEM((tile + 2 * ATTN_BLOCK, D), BF16),
            pltpu.VMEM((tile, 3 * GMLP_WIDTH), F32),
            pltpu.VMEM((tile, ATTN_WIDTH), BF16),
            pltpu.VMEM((tile + 2 * ATTN_BLOCK, 2 * KV_WIDTH), BF16),
            pltpu.VMEM((tile, ATTN_WIDTH), F32),
            pltpu.VMEM((tile, GMLP_WIDTH + ATTN_WIDTH), BF16),
        ],
        compiler_params=pltpu.CompilerParams(
            dimension_semantics=("arbitrary", "arbitrary"),
            vmem_limit_bytes=V7X_VMEM_LIMIT_BYTES),
        name="even_layer",
    )(x, x, x, norm_g.reshape(1, D).astype(F32), w_in.astype(BF16),
      ln_g.reshape(1, -1).astype(F32), ln_b.reshape(1, -1).astype(F32), w_s.astype(BF16), bs_full,
      ab, sink_col, w_out.astype(BF16))


def _hgrn_lower_bound(gamma, layer):
    rows = [gamma[l:l + 1, :] for l in range(DEPTH)]
    mx = functools.reduce(jnp.maximum, rows)
    es = [jnp.exp(r - mx) for r in rows]
    tot = functools.reduce(lambda a, b: a + b, es)
    acc = jnp.zeros_like(tot)
    for l in range(1, layer + 1):
        acc = acc + es[l]
    return acc / tot


def _hgrn_scratch(tile):
    return [pltpu.VMEM((tile, HGRN_WIDTH), F32), pltpu.VMEM((3 * tile, HGRN_WIDTH), BF16)] + [
        pltpu.VMEM((tile, HGRN_WIDTH), BF16) for _ in range(4)]


def _hgrn_pass(proj_ref, lb_ref, state_ref, out_ref, bc_scr, gs_scr, qt_scr, kt_scr, ke_scr, v_scr, *,
               tile, reverse, off_q, off_f, off_i):
    C = HGRN_CHUNK
    HD = HGRN_HEAD_DIM
    nchunks = tile // C
    SUB = 8

    def chunk_total_decay(r0, cols):
        grp = bc_scr[pl.ds(r0 + (0 if reverse else C - SUB), SUB), cols]
        return jnp.exp(grp[0:1] if reverse else grp[SUB - 1:SUB])

    def for_each_piece(fn, unroll):
        def chunk_body(c, carry):
            def head_body(hb, carry2):
                for u in range(unroll):
                    fn(c, hb * unroll + u)
                return carry2

            lax.fori_loop(0, HGRN_HEADS // unroll, head_body, 0)
            return carry

        lax.fori_loop(0, nchunks, chunk_body, 0)

    def gate_piece(c, h):
        r0 = pl.multiple_of(c * C, C)
        hc = pl.multiple_of(h * HD, HD)
        fcols = pl.ds(off_f + hc, HD)
        lb = lb_ref[:, pl.ds(hc, HD)]
        f = lb + (1.0 - lb) * jax.nn.sigmoid(proj_ref[pl.ds(r0, C), fcols])
        proj_ref[pl.ds(r0, C), fcols] = f
        g = jnp.log(f)
        g1 = g.astype(BF16)
        r1 = g - g1.astype(F32)
        g2 = r1.astype(BF16)
        g3 = (r1 - g2.astype(F32)).astype(BF16)
        gs_scr[pl.ds(r0, C), pl.ds(hc, HD)] = g1
        gs_scr[pl.ds(tile + r0, C), pl.ds(hc, HD)] = g2
        gs_scr[pl.ds(2 * tile + r0, C), pl.ds(hc, HD)] = g3

    for_each_piece(gate_piece, 4)

    ti = lax.broadcasted_iota(jnp.int32, (tile, tile), 0)
    si = lax.broadcasted_iota(jnp.int32, (tile, tile), 1)
    tri = (((ti // C) == (si // C)) & ((si >= ti) if reverse else (si <= ti))).astype(BF16)
    bc_scr[...] = jnp.dot(jnp.concatenate([tri, tri, tri], axis=1), gs_scr[...], preferred_element_type=F32)

    def decay_piece(c, h):
        r0 = pl.multiple_of(c * C, C)
        rows = pl.ds(r0, C)
        hc = pl.multiple_of(h * HD, HD)
        cols = pl.ds(hc, HD)
        e = jnp.exp(bc_scr[rows, cols])
        kt = (1.0 - proj_ref[rows, pl.ds(off_f + hc, HD)]) / e
        qt_scr[rows, cols] = (_silu(proj_ref[rows, pl.ds(off_q + hc, HD)]) * e).astype(BF16)
        kt_scr[rows, cols] = kt.astype(BF16)
        ke_scr[rows, cols] = (kt * chunk_total_decay(r0, cols)).astype(BF16)
        v_scr[rows, cols] = proj_ref[rows, pl.ds(off_i + hc, HD)].astype(BF16)

    for_each_piece(decay_piece, 4)

    t2 = lax.broadcasted_iota(jnp.int32, (C, C), 0)
    s2 = lax.broadcasted_iota(jnp.int32, (C, C), 1)
    cmask = (s2 >= t2) if reverse else (s2 <= t2)
    heads_per_body = 4

    def chunk_body(ci, carry):
        c = (nchunks - 1 - ci) if reverse else ci
        r0 = pl.multiple_of(c * C, C)
        rows = pl.ds(r0, C)

        def head_body(hb, carry2):
            for u in range(heads_per_body):
                h = hb * heads_per_body + u
                cols = pl.ds(pl.multiple_of(h * HD, HD), HD)
                qt = qt_scr[rows, cols]
                v = v_scr[rows, cols]
                a = lax.dot_general(qt, kt_scr[rows, cols], _NT, preferred_element_type=F32)
                a = jnp.where(cmask, a, 0.0).astype(BF16)
                st = state_ref[h]
                out_ref[rows, cols] = (jnp.dot(a, v, preferred_element_type=F32)
                                       + lax.dot_general(qt, st.astype(BF16), _NT, preferred_element_type=F32))
                state_ref[h] = (st * chunk_total_decay(r0, cols)
                                + lax.dot_general(v, ke_scr[rows, cols], _TN, preferred_element_type=F32))
            return carry2

        lax.fori_loop(0, HGRN_HEADS // heads_per_body, head_body, 0)
        return carry

    lax.fori_loop(0, nchunks, chunk_body, 0)


def _odd_bwd_kernel(x_ref, ng_ref, w_ref, gamma_ref, ob_ref, proj_scr, lb_scr, state_scr, *hgrn_scr,
                    tile, layer):
    @pl.when(pl.program_id(1) == 0)
    def _():
        state_scr[...] = jnp.zeros_like(state_scr)

    lb_scr[...] = _hgrn_lower_bound(gamma_ref[...], layer)
    hn = _rms_rows(x_ref[0], ng_ref[...]).astype(BF16)
    proj_scr[...] = jnp.dot(hn, w_ref[...], preferred_element_type=F32)
    _hgrn_pass(proj_scr, lb_scr, state_scr, ob_ref.at[0], *hgrn_scr, tile=tile, reverse=True,
               off_q=0, off_f=HGRN_WIDTH, off_i=2 * HGRN_WIDTH)


def _odd_fwd_kernel(x_ref, ob_ref, ng_ref, w_ref, gamma_ref, hng_ref, wout_ref, fg_ref, o_ref,
                    proj_scr, lb_scr, state_scr, of_scr, y_scr, *hgrn_scr, tile, layer):
    @pl.when(pl.program_id(1) == 0)
    def _():
        state_scr[...] = jnp.zeros_like(state_scr)

    lb_scr[...] = _hgrn_lower_bound(gamma_ref[...], layer)
    x = x_ref[0]
    hn = _rms_rows(x, ng_ref[...]).astype(BF16)
    proj_scr[...] = jnp.dot(hn, w_ref[...], preferred_element_type=F32)
    _hgrn_pass(proj_scr, lb_scr, state_scr, of_scr, *hgrn_scr, tile=tile, reverse=False,
               off_q=0, off_f=HGRN_WIDTH, off_i=2 * HGRN_WIDTH)
    for h in range(HGRN_HEADS):
        cs = slice(h * HGRN_HEAD_DIM, (h + 1) * HGRN_HEAD_DIM)
        o = of_scr[:, cs] + ob_ref[0, :, cs]
        o = o * lax.rsqrt(jnp.mean(o * o, axis=-1, keepdims=True) + EPS) * hng_ref[:, cs]
        z = proj_scr[:, 3 * HGRN_WIDTH + h * HGRN_HEAD_DIM:3 * HGRN_WIDTH + (h + 1) * HGRN_HEAD_DIM]
        y_scr[:, cs] = (o * _silu(z)).astype(BF16)
    x2 = x + jnp.dot(y_scr[...], wout_ref[...], preferred_element_type=F32)
    o_ref[0] = _rms_rows(x2, fg_ref[...])


def _odd_layer_and_final_norm(x, norm_g, w_in, gamma_f, gamma_b, head_norm_g, w_out, final_g, layer):
    B, S, D = x.shape
    tile = min(ODD_TILE, S)
    assert S % tile == 0 and tile % HGRN_CHUNK == 0
    nt = S // tile
    W = HGRN_WIDTH
    w_bf = w_in.astype(BF16)
    wq, wff, wfb, wi, wz = (w_bf[:, n * W:(n + 1) * W] for n in range(5))
    w_bwd = jnp.concatenate([wq, wfb, wi], axis=1)
    w_fwd = jnp.concatenate([wq, wff, wi, wz], axis=1)
    ng = norm_g.reshape(1, D).astype(F32)
    params = pltpu.CompilerParams(dimension_semantics=("arbitrary", "arbitrary"),
                                  vmem_limit_bytes=V7X_VMEM_LIMIT_BYTES)

    o_b = pl.pallas_call(
        functools.partial(_odd_bwd_kernel, tile=tile, layer=layer),
        grid=(B, nt),
        in_specs=[
            pl.BlockSpec((1, tile, D), lambda b, t: (b, nt - 1 - t, 0)),
            _const_spec((1, D)),
            _const_spec((D, 3 * W)),
            _const_spec((DEPTH, W)),
        ],
        out_specs=pl.BlockSpec((1, tile, W), lambda b, t: (b, nt - 1 - t, 0)),
        out_shape=jax.ShapeDtypeStruct((B, S, W), F32),
        scratch_shapes=[
            pltpu.VMEM((tile, 3 * W), F32),
            pltpu.VMEM((1, W), F32),
            pltpu.VMEM((HGRN_HEADS, HGRN_HEAD_DIM, HGRN_HEAD_DIM), F32),
        ] + _hgrn_scratch(tile),
        compiler_params=params,
        name="odd_layer_bwd_pass",
    )(x, ng, w_bwd, gamma_b.astype(F32))

    return pl.pallas_call(
        functools.partial(_odd_fwd_kernel, tile=tile, layer=layer),
        grid=(B, nt),
        in_specs=[
            pl.BlockSpec((1, tile, D), lambda b, t: (b, t, 0)),
            pl.BlockSpec((1, tile, W), lambda b, t: (b, t, 0)),
            _const_spec((1, D)),
            _const_spec((D, 4 * W)),
            _const_spec((DEPTH, W)),
            _const_spec((1, W)),
            _const_spec((W, D)),
            _const_spec((1, D)),
        ],
        out_specs=pl.BlockSpec((1, tile, D), lambda b, t: (b, t, 0)),
        out_shape=jax.ShapeDtypeStruct((B, S, D), x.dtype),
        scratch_shapes=[
            pltpu.VMEM((tile, 4 * W), F32),
            pltpu.VMEM((1, W), F32),
            pltpu.VMEM((HGRN_HEADS, HGRN_HEAD_DIM, HGRN_HEAD_DIM), F32),
            pltpu.VMEM((tile, W), F32),
            pltpu.VMEM((tile, W), BF16),
        ] + _hgrn_scratch(tile),
        compiler_params=params,
        name="odd_layer_fwd_pass",
    )(x, o_b, ng, w_fwd, gamma_f.astype(F32), head_norm_g.reshape(1, W).astype(F32),
      w_out.astype(BF16), final_g.reshape(1, D).astype(F32))


def kernel(x, norm_g_even, w_in_even, gmlp_ln_g, gmlp_ln_b, gmlp_w_s, gmlp_b_s, attn_sink, w_out_even, norm_g_odd, w_in_odd, hgrn_gamma_fwd, hgrn_gamma_bwd, hgrn_head_norm_g, w_out_odd, final_norm_g):
    assert DEPTH == 2
    x = _even_layer(x, norm_g_even[0], w_in_even[0], gmlp_ln_g[0], gmlp_ln_b[0], gmlp_w_s[0], gmlp_b_s[0],
                    attn_sink[0], w_out_even[0])
    return _odd_layer_and_final_norm(x, norm_g_odd[0], w_in_odd[0], hgrn_gamma_fwd, hgrn_gamma_bwd,
                                     hgrn_head_norm_g[0], w_out_odd[0], final_norm_g, layer=1)
```

```python
import functools

import jax
import jax.numpy as jnp
import numpy as np
from jax import lax
from jax.experimental import pallas as pl
from jax.experimental.pallas import tpu as pltpu

F32 = jnp.float32
BF16 = jnp.bfloat16

D_MODEL = 1024
DEPTH = 2
MIX_WIDTH = 2 * D_MODEL
GMLP_WIDTH = MIX_WIDTH // 2
GMLP_GROUPS = 4
GMLP_GROUP_DIM = GMLP_WIDTH // GMLP_GROUPS
GMLP_CHUNK = 128
ATTN_WIDTH = MIX_WIDTH // 2
HEAD_DIM = 128
N_Q_HEADS = ATTN_WIDTH // HEAD_DIM
N_KV_HEADS = 2
GQA_GROUP = N_Q_HEADS // N_KV_HEADS
KV_WIDTH = N_KV_HEADS * HEAD_DIM
WINDOW = 128
ATTN_BLOCK = 128
HGRN_WIDTH = MIX_WIDTH
HGRN_HEAD_DIM = 128
HGRN_HEADS = HGRN_WIDTH // HGRN_HEAD_DIM
HGRN_CHUNK = 128
EPS = 1e-6
IN_EVEN = 3 * GMLP_WIDTH + ATTN_WIDTH + 2 * KV_WIDTH + ATTN_WIDTH

_OFF_UVZ = 0
_OFF_Q = 3 * GMLP_WIDTH
_OFF_KV = _OFF_Q + ATTN_WIDTH
_OFF_ZB = _OFF_KV + 2 * KV_WIDTH

F32_SUBLANES = 8

V7X_VMEM_LIMIT_BYTES = 58 * 1024 * 1024

EVEN_TILE = 512
ODD_TILE = 256

_NT = (((1,), (1,)), ((), ()))
_TN = (((0,), (0,)), ((), ()))


def _const_spec(shape):
    nd = len(shape)
    return pl.BlockSpec(shape, lambda *_: (0,) * nd, pipeline_mode=pl.Buffered(1))


def _rms_rows(x, g):
    return x * lax.rsqrt(jnp.mean(x * x, axis=-1, keepdims=True) + EPS) * g


def _sigmoid(z):
    return 0.5 * jnp.tanh(0.5 * z) + 0.5


def _silu(z):
    return z * _sigmoid(z)


def _even_kernel(x_ref, xp_ref, xn_ref, ng_ref, win_ref, lng_ref, lnb_ref, ws_ref, bs_ref,
                 ab_ref, sink_ref, wout_ref, o_ref,
                 hn_scr, uvz_scr, q_scr, kv_scr, zb_scr, y_scr, *, tile, seq):
    t = pl.program_id(1)
    ng = ng_ref[...]
    hn_scr[0:ATTN_BLOCK, :] = _rms_rows(xp_ref[0], ng).astype(BF16)
    hn_scr[ATTN_BLOCK:ATTN_BLOCK + tile, :] = _rms_rows(x_ref[0], ng).astype(BF16)
    hn_scr[ATTN_BLOCK + tile:, :] = _rms_rows(xn_ref[0], ng).astype(BF16)

    hn_cur = hn_scr[ATTN_BLOCK:ATTN_BLOCK + tile, :]
    uvz_scr[...] = jnp.dot(hn_cur, win_ref[:, _OFF_UVZ:_OFF_Q], preferred_element_type=F32)
    q_scr[...] = (jnp.dot(hn_cur, win_ref[:, _OFF_Q:_OFF_KV], preferred_element_type=F32)
                  * (HEAD_DIM ** -0.5)).astype(BF16)
    kv_scr[...] = jnp.dot(hn_scr[...], win_ref[:, _OFF_KV:_OFF_ZB], preferred_element_type=F32).astype(BF16)
    zb_scr[...] = jnp.dot(hn_cur, win_ref[:, _OFF_ZB:], preferred_element_type=F32)

    lng = lng_ref[...]
    lnb = lnb_ref[...]
    kj = lax.broadcasted_iota(jnp.int32, (1, 3 * ATTN_BLOCK), 1)

    def block_body(j, carry):
        r0 = j * ATTN_BLOCK
        rows = pl.ds(r0, ATTN_BLOCK)
        v = uvz_scr[rows, GMLP_WIDTH:2 * GMLP_WIDTH]
        mu = jnp.mean(v, axis=-1, keepdims=True)
        vc = v - mu
        var = jnp.mean(vc * vc, axis=-1, keepdims=True)
        vn = (vc * lax.rsqrt(var + EPS) * lng + lnb).astype(BF16)
        mixed = jnp.concatenate(
            [jnp.dot(ws_ref[g], vn[:, g * GMLP_GROUP_DIM:(g + 1) * GMLP_GROUP_DIM], preferred_element_type=F32)
             for g in range(GMLP_GROUPS)], axis=1) + bs_ref[...]
        u = uvz_scr[rows, 0:GMLP_WIDTH]
        za = uvz_scr[rows, 2 * GMLP_WIDTH:3 * GMLP_WIDTH]
        y_scr[rows, 0:GMLP_WIDTH] = (u * mixed * _silu(za)).astype(BF16)

        kpos = t * tile + r0 - ATTN_BLOCK + kj
        in_seq = (kpos >= 0) & (kpos < seq)
        win = pl.ds(r0, 3 * ATTN_BLOCK)
        for hk in range(N_KV_HEADS):
            qs = jnp.concatenate(
                [q_scr[rows, (hk * GQA_GROUP + g) * HEAD_DIM:(hk * GQA_GROUP + g + 1) * HEAD_DIM]
                 for g in range(GQA_GROUP)], axis=0)
            kw = kv_scr[win, hk * HEAD_DIM:(hk + 1) * HEAD_DIM]
            vw = kv_scr[win, KV_WIDTH + hk * HEAD_DIM:KV_WIDTH + (hk + 1) * HEAD_DIM]
            s = lax.dot_general(qs, kw, _NT, preferred_element_type=F32) + ab_ref[hk]
            s = jnp.where(in_seq, s, -jnp.inf)
            sink = sink_ref[hk]
            m = jnp.maximum(jnp.max(s, axis=-1, keepdims=True), sink)
            p = jnp.exp(s - m)
            den = jnp.sum(p, axis=-1, keepdims=True) + jnp.exp(sink - m)
            o = jnp.dot(p.astype(BF16), vw, preferred_element_type=F32) / den
            for g in range(GQA_GROUP):
                c0 = (hk * GQA_GROUP + g) * HEAD_DIM
                zb = zb_scr[rows, c0:c0 + HEAD_DIM]
                y_scr[rows, GMLP_WIDTH + c0:GMLP_WIDTH + c0 + HEAD_DIM] = (
                    o[g * ATTN_BLOCK:(g + 1) * ATTN_BLOCK] * _silu(zb)).astype(BF16)
        return carry

    for j in range(tile // ATTN_BLOCK):
        block_body(j, 0)
    o_ref[0] = x_ref[0] + jnp.dot(y_scr[...], wout_ref[...], preferred_element_type=F32)


def _even_layer(x, norm_g, w_in, ln_g, ln_b, w_s, b_s, sink, w_out):
    B, S, D = x.shape
    tile = min(EVEN_TILE, S)
    assert S % tile == 0 and tile % ATTN_BLOCK == 0
    nb = S // ATTN_BLOCK
    tpb = tile // ATTN_BLOCK

    qi = np.arange(ATTN_BLOCK)
    kj = np.arange(3 * ATTN_BLOCK) - ATTN_BLOCK
    dist = np.abs(kj[None, :] - qi[:, None]).astype(np.float32)
    slopes = np.exp2(-8.0 * np.arange(1, N_Q_HEADS + 1, dtype=np.float32) / N_Q_HEADS)
    ab = np.where(dist[None] <= WINDOW, -slopes[:, None, None] * dist[None], -np.inf).astype(np.float32)
    ab = jnp.asarray(ab.reshape(N_KV_HEADS, GQA_GROUP * ATTN_BLOCK, 3 * ATTN_BLOCK))
    sink_col = jnp.broadcast_to(sink.astype(F32).reshape(N_KV_HEADS, GQA_GROUP, 1, 1),
                                (N_KV_HEADS, GQA_GROUP, ATTN_BLOCK, 1)).reshape(N_KV_HEADS, GQA_GROUP * ATTN_BLOCK, 1)
    bs_full = jnp.repeat(b_s.astype(F32).T, GMLP_GROUP_DIM, axis=1)

    kern = functools.partial(_even_kernel, tile=tile, seq=S)
    return pl.pallas_call(
        kern,
        grid=(B, S // tile),
        in_specs=[
            pl.BlockSpec((1, tile, D), lambda b, t: (b, t, 0)),
            pl.BlockSpec((1, ATTN_BLOCK, D), lambda b, t: (b, jnp.maximum(t * tpb - 1, 0), 0)),
            pl.BlockSpec((1, ATTN_BLOCK, D), lambda b, t: (b, jnp.minimum((t + 1) * tpb, nb - 1), 0)),
            _const_spec((1, D)),
            _const_spec((D, IN_EVEN)),
            _const_spec((1, GMLP_WIDTH)),
            _const_spec((1, GMLP_WIDTH)),
            _const_spec((GMLP_GROUPS, GMLP_CHUNK, GMLP_CHUNK)),
            _const_spec((GMLP_CHUNK, GMLP_WIDTH)),
            _const_spec((N_KV_HEADS, GQA_GROUP * ATTN_BLOCK, 3 * ATTN_BLOCK)),
            _const_spec((N_KV_HEADS, GQA_GROUP * ATTN_BLOCK, 1)),
            _const_spec((GMLP_WIDTH + ATTN_WIDTH, D)),
        ],
        out_specs=pl.BlockSpec((1, tile, D), lambda b, t: (b, t, 0)),
        out_shape=jax.ShapeDtypeStruct((B, S, D), x.dtype),
        scratch_shapes=[
            pltpu.VMEM((tile + 2 * ATTN_BLOCK, D), BF16),
            pltpu.VMEM((tile, 3 * GMLP_WIDTH), F32),
            pltpu.VMEM((tile, ATTN_WIDTH), BF16),
            pltpu.VMEM((tile + 2 * ATTN_BLOCK, 2 * KV_WIDTH), BF16),
            pltpu.VMEM((tile, ATTN_WIDTH), F32),
            pltpu.VMEM((tile, GMLP_WIDTH + ATTN_WIDTH), BF16),
        ],
        compiler_params=pltpu.CompilerParams(
            dimension_semantics=("arbitrary", "arbitrary"),
            vmem_limit_bytes=V7X_VMEM_LIMIT_BYTES),
        name="even_layer",
    )(x, x, x, norm_g.reshape(1, D).astype(F32), w_in.astype(BF16),
      ln_g.reshape(1, -1).astype(F32), ln_b.reshape(1, -1).astype(F32), w_s.astype(BF16), bs_full,
      ab, sink_col, w_out.astype(BF16))


def _hgrn_lower_bound(gamma, layer):
    rows = [gamma[l:l + 1, :] for l in range(DEPTH)]
    mx = functools.reduce(jnp.maximum, rows)
    es = [jnp.exp(r - mx) for r in rows]
    tot = functools.reduce(lambda a, b: a + b, es)
    acc = jnp.zeros_like(tot)
    for l in range(1, layer + 1):
        acc = acc + es[l]
    return acc / tot


def _hgrn_scratch(tile):
    W = HGRN_WIDTH
    return [
        pltpu.VMEM((tile, W), F32),
        pltpu.VMEM((3 * tile, W), BF16),
        pltpu.VMEM((tile, W), BF16),
        pltpu.VMEM((tile, W), BF16),
        pltpu.VMEM((tile, W), BF16),
        pltpu.VMEM((W, tile), BF16),
        pltpu.VMEM((W, tile), BF16),
        pltpu.VMEM((tile // HGRN_CHUNK * F32_SUBLANES, W), F32),
    ]


def _hgrn_pass(proj_ref, lb_ref, state_ref, out_ref, r_scr, gs_scr, qt_scr, qs_scr, v_scr, ktT_scr, keT_scr, dn_scr,
               *, tile, reverse, off_q, off_f, off_i):
    C = HGRN_CHUNK
    HD = HGRN_HEAD_DIM
    P = C // 2
    SUB = F32_SUBLANES
    nchunks = tile // C
    kt_scr = gs_scr.at[pl.ds(0, tile)]
    ke_scr = gs_scr.at[pl.ds(tile, tile)]

    def for_each_piece(load, compute_store, nrow, unroll):
        def row_body(rb, carry):
            def head_body(hb, carry2):
                loaded = [load(rb, hb * unroll + u) for u in range(unroll)]
                for u in range(unroll):
                    compute_store(rb, hb * unroll + u, loaded[u])
                return carry2

            lax.fori_loop(0, HGRN_HEADS // unroll, head_body, 0)
            return carry

        lax.fori_loop(0, nrow, row_body, 0)

    def gate_load(rb, h):
        return proj_ref[pl.ds(pl.multiple_of(rb * P, P), P), pl.ds(off_f + pl.multiple_of(h * HD, HD), HD)]

    def gate_piece(rb, h, logit):
        r0 = pl.multiple_of(rb * P, P)
        hc = pl.multiple_of(h * HD, HD)
        lb = lb_ref[:, pl.ds(hc, HD)]
        f = lb + (1.0 - lb) * _sigmoid(logit)
        proj_ref[pl.ds(r0, P), pl.ds(off_f + hc, HD)] = f
        g = jnp.log(f)
        g1 = g.astype(BF16)
        r1 = g - g1.astype(F32)
        g2 = r1.astype(BF16)
        g3 = (r1 - g2.astype(F32)).astype(BF16)
        gs_scr[pl.ds(r0, P), pl.ds(hc, HD)] = g1
        gs_scr[pl.ds(tile + r0, P), pl.ds(hc, HD)] = g2
        gs_scr[pl.ds(2 * tile + r0, P), pl.ds(hc, HD)] = g3

    for_each_piece(gate_load, gate_piece, tile // P, 4)

    ti = lax.broadcasted_iota(jnp.int32, (tile, tile), 0)
    si = lax.broadcasted_iota(jnp.int32, (tile, tile), 1)
    same_chunk = (ti // C) == (si // C)
    if reverse:
        coef = (si >= ti).astype(jnp.int32) - ((si % C) >= P).astype(jnp.int32)
    else:
        coef = (si <= ti).astype(jnp.int32) - ((si % C) < P).astype(jnp.int32)
    scan = jnp.where(same_chunk, coef, 0).astype(F32).astype(BF16)
    r_scr[...] = jnp.dot(jnp.concatenate([scan, scan, scan], axis=1), gs_scr[...], preferred_element_type=F32)

    near_grp, near_row = ((C - SUB, SUB - 1) if reverse else (0, 0))
    far_grp, far_row = ((0, 0) if reverse else (C - SUB, SUB - 1))

    def decay_load(rb, h):
        rows = pl.ds(pl.multiple_of(rb * P, P), P)
        c0 = pl.multiple_of(lax.div(rb, 2) * C, C)
        hc = pl.multiple_of(h * HD, HD)
        cols = pl.ds(hc, HD)
        fcols = pl.ds(off_f + hc, HD)
        return (r_scr[rows, cols], proj_ref[rows, fcols],
                proj_ref[rows, pl.ds(off_q + hc, HD)], proj_ref[rows, pl.ds(off_i + hc, HD)],
                r_scr[pl.ds(c0 + near_grp, SUB), cols], proj_ref[pl.ds(c0 + near_grp, SUB), fcols],
                r_scr[pl.ds(c0 + far_grp, SUB), cols])

    def decay_piece(rb, h, loaded):
        r, f, q, i, r_near, f_near, r_far = loaded
        rows = pl.ds(pl.multiple_of(rb * P, P), P)
        cols = pl.ds(pl.multiple_of(h * HD, HD), HD)
        e = jnp.exp(r)
        e_mid = f_near[near_row:near_row + 1] / jnp.exp(r_near[near_row:near_row + 1])
        e_far = jnp.exp(r_far[far_row:far_row + 1])
        qt = _silu(q) * e
        kt = (1.0 - f) / e
        qt_scr[rows, cols] = qt.astype(BF16)
        qs_scr[rows, cols] = (qt * e_mid).astype(BF16)
        kt_scr[rows, cols] = kt.astype(BF16)
        ke_scr[rows, cols] = (kt * e_far).astype(BF16)
        v_scr[rows, cols] = i.astype(BF16)
        dn_scr[pl.ds(pl.multiple_of(lax.div(rb, 2) * SUB, SUB), SUB), cols] = jnp.broadcast_to(e_mid * e_far, (SUB, HD))

    for_each_piece(decay_load, decay_piece, tile // P, 4)

    def xpose_load(c, h):
        rows = pl.ds(pl.multiple_of(c * C, C), C)
        cols = pl.ds(pl.multiple_of(h * HD, HD), HD)
        return kt_scr[rows, cols], ke_scr[rows, cols]

    def xpose_piece(c, h, loaded):
        kt, ke = loaded
        drows = pl.ds(pl.multiple_of(h * HD, HD), HD)
        tcols = pl.ds(pl.multiple_of(c * C, C), C)
        ktT_scr[drows, tcols] = kt.T
        keT_scr[drows, tcols] = ke.T

    for_each_piece(xpose_load, xpose_piece, nchunks, 4)

    t2 = lax.broadcasted_iota(jnp.int32, (C, C), 0)
    s2 = lax.broadcasted_iota(jnp.int32, (C, C), 1)
    cmask = (s2 >= t2) if reverse else (s2 <= t2)
    def chunk_body(ci, carry):
        c = (nchunks - 1 - ci) if reverse else ci
        rows = pl.ds(pl.multiple_of(c * C, C), C)
        dn_rows = pl.ds(pl.multiple_of(c * SUB, SUB), SUB)
        for h in range(HGRN_HEADS):
            cols = slice(h * HD, (h + 1) * HD)
            st = state_ref[h]
            v = v_scr[rows, cols]
            a = jnp.dot(qt_scr[rows, cols], ktT_scr[cols, rows], preferred_element_type=F32)
            a = jnp.where(cmask, a, 0.0).astype(BF16)
            out_ref[rows, cols] = jnp.dot(jnp.concatenate([a, qs_scr[rows, cols]], axis=1),
                                          jnp.concatenate([v, st.astype(BF16)], axis=0),
                                          preferred_element_type=F32)
            decay = jnp.broadcast_to(dn_scr[dn_rows, cols][0:1], (HD, HD)).T
            state_ref[h] = st * decay + jnp.dot(keT_scr[cols, rows], v, preferred_element_type=F32)
        return carry

    lax.fori_loop(0, nchunks, chunk_body, 0)


def _odd_bwd_kernel(x_ref, ng_ref, w_ref, gamma_ref, ob_ref, proj_scr, lb_scr, state_scr, *hgrn_scr,
                    tile, layer):
    @pl.when(pl.program_id(1) == 0)
    def _():
        state_scr[...] = jnp.zeros_like(state_scr)

    lb_scr[...] = _hgrn_lower_bound(gamma_ref[...], layer)
    hn = _rms_rows(x_ref[0], ng_ref[...]).astype(BF16)
    proj_scr[...] = jnp.dot(hn, w_ref[...], preferred_element_type=F32)
    _hgrn_pass(proj_scr, lb_scr, state_scr, ob_ref.at[0], *hgrn_scr, tile=tile, reverse=True,
               off_q=0, off_f=HGRN_WIDTH, off_i=2 * HGRN_WIDTH)


def _odd_fwd_kernel(x_ref, ob_ref, ng_ref, w_ref, gamma_ref, hng_ref, wout_ref, fg_ref, o_ref,
                    proj_scr, lb_scr, state_scr, of_scr, y_scr, *hgrn_scr, tile, layer):
    @pl.when(pl.program_id(1) == 0)
    def _():
        state_scr[...] = jnp.zeros_like(state_scr)

    lb_scr[...] = _hgrn_lower_bound(gamma_ref[...], layer)
    x = x_ref[0]
    hn = _rms_rows(x, ng_ref[...]).astype(BF16)
    proj_scr[...] = jnp.dot(hn, w_ref[...], preferred_element_type=F32)
    _hgrn_pass(proj_scr, lb_scr, state_scr, of_scr, *hgrn_scr, tile=tile, reverse=False,
               off_q=0, off_f=HGRN_WIDTH, off_i=2 * HGRN_WIDTH)
    for h in range(HGRN_HEADS):
        cs = slice(h * HGRN_HEAD_DIM, (h + 1) * HGRN_HEAD_DIM)
        o = of_scr[:, cs] + ob_ref[0, :, cs]
        o = o * lax.rsqrt(jnp.mean(o * o, axis=-1, keepdims=True) + EPS) * hng_ref[:, cs]
        z = proj_scr[:, 3 * HGRN_WIDTH + h * HGRN_HEAD_DIM:3 * HGRN_WIDTH + (h + 1) * HGRN_HEAD_DIM]
        y_scr[:, cs] = (o * _silu(z)).astype(BF16)
    x2 = x + jnp.dot(y_scr[...], wout_ref[...], preferred_element_type=F32)
    o_ref[0] = _rms_rows(x2, fg_ref[...])


def _odd_layer_and_final_norm(x, norm_g, w_in, gamma_f, gamma_b, head_norm_g, w_out, final_g, layer):
    B, S, D = x.shape
    tile = min(ODD_TILE, S)
    assert S % tile == 0 and tile % HGRN_CHUNK == 0
    nt = S // tile
    W = HGRN_WIDTH
    w_bf = w_in.astype(BF16)
    wq, wff, wfb, wi, wz = (w_bf[:, n * W:(n + 1) * W] for n in range(5))
    w_bwd = jnp.concatenate([wq, wfb, wi], axis=1)
    w_fwd = jnp.concatenate([wq, wff, wi, wz], axis=1)
    ng = norm_g.reshape(1, D).astype(F32)
    params = pltpu.CompilerParams(dimension_semantics=("arbitrary", "arbitrary"),
                                  vmem_limit_bytes=V7X_VMEM_LIMIT_BYTES)

    o_b = pl.pallas_call(
        functools.partial(_odd_bwd_kernel, tile=tile, layer=layer),
        grid=(B, nt),
        in_specs=[
            pl.BlockSpec((1, tile, D), lambda b, t: (b, nt - 1 - t, 0)),
            _const_spec((1, D)),
            _const_spec((D, 3 * W)),
            _const_spec((DEPTH, W)),
        ],
        out_specs=pl.BlockSpec((1, tile, W), lambda b, t: (b, nt - 1 - t, 0)),
        out_shape=jax.ShapeDtypeStruct((B, S, W), F32),
        scratch_shapes=[
            pltpu.VMEM((tile, 3 * W), F32),
            pltpu.VMEM((1, W), F32),
            pltpu.VMEM((HGRN_HEADS, HGRN_HEAD_DIM, HGRN_HEAD_DIM), F32),
        ] + _hgrn_scratch(tile),
        compiler_params=params,
        name="odd_layer_bwd_pass",
    )(x, ng, w_bwd, gamma_b.astype(F32))

    return pl.pallas_call(
        functools.partial(_odd_fwd_kernel, tile=tile, layer=layer),
        grid=(B, nt),
        in_specs=[
            pl.BlockSpec((1, tile, D), lambda b, t: (b, t, 0)),
            pl.BlockSpec((1, tile, W), lambda b, t: (b, t, 0)),
            _const_spec((1, D)),
            _const_spec((D, 4 * W)),
            _const_spec((DEPTH, W)),
            _const_spec((1, W)),
            _const_spec((W, D)),
            _const_spec((1, D)),
        ],
        out_specs=pl.BlockSpec((1, tile, D), lambda b, t: (b, t, 0)),
        out_shape=jax.ShapeDtypeStruct((B, S, D), x.dtype),
        scratch_shapes=[
            pltpu.VMEM((tile, 4 * W), F32),
            pltpu.VMEM((1, W), F32),
            pltpu.VMEM((HGRN_HEADS, HGRN_HEAD_DIM, HGRN_HEAD_DIM), F32),
            pltpu.VMEM((tile, W), F32),
            pltpu.VMEM((tile, W), BF16),
        ] + _hgrn_scratch(tile),
        compiler_params=params,
        name="odd_layer_fwd_pass",
    )(x, o_b, ng, w_fwd, gamma_f.astype(F32), head_norm_g.reshape(1, W).astype(F32),
      w_out.astype(BF16), final_g.reshape(1, D).astype(F32))


def kernel(x, norm_g_even, w_in_even, gmlp_ln_g, gmlp_ln_b, gmlp_w_s, gmlp_b_s, attn_sink, w_out_even, norm_g_odd, w_in_odd, hgrn_gamma_fwd, hgrn_gamma_bwd, hgrn_head_norm_g, w_out_odd, final_norm_g):
    assert DEPTH == 2
    x = _even_layer(x, norm_g_even[0], w_in_even[0], gmlp_ln_g[0], gmlp_ln_b[0], gmlp_w_s[0], gmlp_b_s[0],
                    attn_sink[0], w_out_even[0])
    return _odd_layer_and_final_norm(x, norm_g_odd[0], w_in_odd[0], hgrn_gamma_fwd, hgrn_gamma_bwd,
                                     hgrn_head_norm_g[0], w_out_odd[0], final_norm_g, layer=1)
```

```python
import functools

import jax
import jax.numpy as jnp
import numpy as np
from jax import lax
from jax.experimental import pallas as pl
from jax.experimental.pallas import tpu as pltpu

F32 = jnp.float32
BF16 = jnp.bfloat16

D_MODEL = 1024
DEPTH = 2
MIX_WIDTH = 2 * D_MODEL
GMLP_WIDTH = MIX_WIDTH // 2
GMLP_GROUPS = 4
GMLP_GROUP_DIM = GMLP_WIDTH // GMLP_GROUPS
GMLP_CHUNK = 128
ATTN_WIDTH = MIX_WIDTH // 2
HEAD_DIM = 128
N_Q_HEADS = ATTN_WIDTH // HEAD_DIM
N_KV_HEADS = 2
GQA_GROUP = N_Q_HEADS // N_KV_HEADS
KV_WIDTH = N_KV_HEADS * HEAD_DIM
WINDOW = 128
ATTN_BLOCK = 128
HGRN_WIDTH = MIX_WIDTH
HGRN_HEAD_DIM = 128
HGRN_HEADS = HGRN_WIDTH // HGRN_HEAD_DIM
HGRN_CHUNK = 128
HGRN_GROUP_HEADS = 2
EPS = 1e-6
IN_EVEN = 3 * GMLP_WIDTH + ATTN_WIDTH + 2 * KV_WIDTH + ATTN_WIDTH

_OFF_UVZ = 0
_OFF_Q = 3 * GMLP_WIDTH
_OFF_KV = _OFF_Q + ATTN_WIDTH
_OFF_ZB = _OFF_KV + 2 * KV_WIDTH

F32_SUBLANES = 8

V7X_VMEM_LIMIT_BYTES = 58 * 1024 * 1024

EVEN_TILE = 512
ODD_TILE = 256

_NT = (((1,), (1,)), ((), ()))


def _const_spec(shape):
    nd = len(shape)
    return pl.BlockSpec(shape, lambda *_: (0,) * nd, pipeline_mode=pl.Buffered(1))


def _rms_rows(x, g):
    return x * lax.rsqrt(jnp.mean(x * x, axis=-1, keepdims=True) + EPS) * g


def _sigmoid(z):
    return 0.5 * jnp.tanh(0.5 * z) + 0.5


def _silu(z):
    return z * _sigmoid(z)


def _even_kernel(x_ref, xp_ref, xn_ref, ng_ref, win_ref, lng_ref, lnb_ref, ws_ref, bs_ref,
                 ab_ref, sink_ref, wout_ref, o_ref,
                 hn_scr, uvz_scr, q_scr, kv_scr, zb_scr, y_scr, *, tile, seq):
    t = pl.program_id(1)
    ng = ng_ref[...]
    hn_scr[0:ATTN_BLOCK, :] = _rms_rows(xp_ref[0], ng).astype(BF16)
    hn_scr[ATTN_BLOCK:ATTN_BLOCK + tile, :] = _rms_rows(x_ref[0], ng).astype(BF16)
    hn_scr[ATTN_BLOCK + tile:, :] = _rms_rows(xn_ref[0], ng).astype(BF16)

    hn_cur = hn_scr[ATTN_BLOCK:ATTN_BLOCK + tile, :]
    uvz_scr[...] = jnp.dot(hn_cur, win_ref[:, _OFF_UVZ:_OFF_Q], preferred_element_type=F32)
    q_scr[...] = (jnp.dot(hn_cur, win_ref[:, _OFF_Q:_OFF_KV], preferred_element_type=F32)
                  * (HEAD_DIM ** -0.5)).astype(BF16)
    kv_scr[...] = jnp.dot(hn_scr[...], win_ref[:, _OFF_KV:_OFF_ZB], preferred_element_type=F32).astype(BF16)
    zb_scr[...] = jnp.dot(hn_cur, win_ref[:, _OFF_ZB:], preferred_element_type=F32)

    lng = lng_ref[...]
    lnb = lnb_ref[...]
    kj = lax.broadcasted_iota(jnp.int32, (1, 3 * ATTN_BLOCK), 1)

    for j in range(tile // ATTN_BLOCK):
        r0 = j * ATTN_BLOCK
        rows = slice(r0, r0 + ATTN_BLOCK)
        v = uvz_scr[rows, GMLP_WIDTH:2 * GMLP_WIDTH]
        mu = jnp.mean(v, axis=-1, keepdims=True)
        vc = v - mu
        var = jnp.mean(vc * vc, axis=-1, keepdims=True)
        vn = (vc * lax.rsqrt(var + EPS) * lng + lnb).astype(BF16)
        mixed = jnp.concatenate(
            [jnp.dot(ws_ref[g], vn[:, g * GMLP_GROUP_DIM:(g + 1) * GMLP_GROUP_DIM], preferred_element_type=F32)
             for g in range(GMLP_GROUPS)], axis=1) + bs_ref[...]
        u = uvz_scr[rows, 0:GMLP_WIDTH]
        za = uvz_scr[rows, 2 * GMLP_WIDTH:3 * GMLP_WIDTH]
        y_scr[rows, 0:GMLP_WIDTH] = (u * mixed * _silu(za)).astype(BF16)

        kpos = t * tile + r0 - ATTN_BLOCK + kj
        in_seq = (kpos >= 0) & (kpos < seq)
        win = slice(r0, r0 + 3 * ATTN_BLOCK)
        for hk in range(N_KV_HEADS):
            qs = jnp.concatenate(
                [q_scr[rows, (hk * GQA_GROUP + g) * HEAD_DIM:(hk * GQA_GROUP + g + 1) * HEAD_DIM]
                 for g in range(GQA_GROUP)], axis=0)
            kw = kv_scr[win, hk * HEAD_DIM:(hk + 1) * HEAD_DIM]
            vw = kv_scr[win, KV_WIDTH + hk * HEAD_DIM:KV_WIDTH + (hk + 1) * HEAD_DIM]
            s = lax.dot_general(qs, kw, _NT, preferred_element_type=F32) + ab_ref[hk]
            s = jnp.where(in_seq, s, -jnp.inf)
            sink = sink_ref[hk]
            m = jnp.maximum(jnp.max(s, axis=-1, keepdims=True), sink)
            p = jnp.exp(s - m)
            den = jnp.sum(p, axis=-1, keepdims=True) + jnp.exp(sink - m)
            o = jnp.dot(p.astype(BF16), vw, preferred_element_type=F32) / den
            for g in range(GQA_GROUP):
                c0 = (hk * GQA_GROUP + g) * HEAD_DIM
                zb = zb_scr[rows, c0:c0 + HEAD_DIM]
                y_scr[rows, GMLP_WIDTH + c0:GMLP_WIDTH + c0 + HEAD_DIM] = (
                    o[g * ATTN_BLOCK:(g + 1) * ATTN_BLOCK] * _silu(zb)).astype(BF16)

    o_ref[0] = x_ref[0] + jnp.dot(y_scr[...], wout_ref[...], preferred_element_type=F32)


def _even_layer(x, norm_g, w_in, ln_g, ln_b, w_s, b_s, sink, w_out):
    B, S, D = x.shape
    tile = min(EVEN_TILE, S)
    assert S % tile == 0 and tile % ATTN_BLOCK == 0
    nb = S // ATTN_BLOCK
    tpb = tile // ATTN_BLOCK

    qi = np.arange(ATTN_BLOCK)
    kj = np.arange(3 * ATTN_BLOCK) - ATTN_BLOCK
    dist = np.abs(kj[None, :] - qi[:, None]).astype(np.float32)
    slopes = np.exp2(-8.0 * np.arange(1, N_Q_HEADS + 1, dtype=np.float32) / N_Q_HEADS)
    ab = np.where(dist[None] <= WINDOW, -slopes[:, None, None] * dist[None], -np.inf).astype(np.float32)
    ab = jnp.asarray(ab.reshape(N_KV_HEADS, GQA_GROUP * ATTN_BLOCK, 3 * ATTN_BLOCK))
    sink_col = jnp.broadcast_to(sink.astype(F32).reshape(N_KV_HEADS, GQA_GROUP, 1, 1),
                                (N_KV_HEADS, GQA_GROUP, ATTN_BLOCK, 1)).reshape(N_KV_HEADS, GQA_GROUP * ATTN_BLOCK, 1)
    bs_full = jnp.repeat(b_s.astype(F32).T, GMLP_GROUP_DIM, axis=1)

    kern = functools.partial(_even_kernel, tile=tile, seq=S)
    return pl.pallas_call(
        kern,
        grid=(B, S // tile),
        in_specs=[
            pl.BlockSpec((1, tile, D), lambda b, t: (b, t, 0)),
            pl.BlockSpec((1, ATTN_BLOCK, D), lambda b, t: (b, jnp.maximum(t * tpb - 1, 0), 0)),
            pl.BlockSpec((1, ATTN_BLOCK, D), lambda b, t: (b, jnp.minimum((t + 1) * tpb, nb - 1), 0)),
            _const_spec((1, D)),
            _const_spec((D, IN_EVEN)),
            _const_spec((1, GMLP_WIDTH)),
            _const_spec((1, GMLP_WIDTH)),
            _const_spec((GMLP_GROUPS, GMLP_CHUNK, GMLP_CHUNK)),
            _const_spec((GMLP_CHUNK, GMLP_WIDTH)),
            _const_spec((N_KV_HEADS, GQA_GROUP * ATTN_BLOCK, 3 * ATTN_BLOCK)),
            _const_spec((N_KV_HEADS, GQA_GROUP * ATTN_BLOCK, 1)),
            _const_spec((GMLP_WIDTH + ATTN_WIDTH, D)),
        ],
        out_specs=pl.BlockSpec((1, tile, D), lambda b, t: (b, t, 0)),
        out_shape=jax.ShapeDtypeStruct((B, S, D), x.dtype),
        scratch_shapes=[
            pltpu.VMEM((tile + 2 * ATTN_BLOCK, D), BF16),
            pltpu.VMEM((tile, 3 * GMLP_WIDTH), F32),
            pltpu.VMEM((tile, ATTN_WIDTH), BF16),
            pltpu.VMEM((tile + 2 * ATTN_BLOCK, 2 * KV_WIDTH), BF16),
            pltpu.VMEM((tile, ATTN_WIDTH), F32),
            pltpu.VMEM((tile, GMLP_WIDTH + ATTN_WIDTH), BF16),
        ],
        compiler_params=pltpu.CompilerParams(
            dimension_semantics=("arbitrary", "arbitrary"),
            vmem_limit_bytes=V7X_VMEM_LIMIT_BYTES),
        name="even_layer",
    )(x, x, x, norm_g.reshape(1, D).astype(F32), w_in.astype(BF16),
      ln_g.reshape(1, -1).astype(F32), ln_b.reshape(1, -1).astype(F32), w_s.astype(BF16), bs_full,
      ab, sink_col, w_out.astype(BF16))


def _hgrn_lower_bound(gamma, layer):
    rows = [gamma[l:l + 1, :] for l in range(DEPTH)]
    mx = functools.reduce(jnp.maximum, rows)
    es = [jnp.exp(r - mx) for r in rows]
    tot = functools.reduce(lambda a, b: a + b, es)
    acc = jnp.zeros_like(tot)
    for l in range(1, layer + 1):
        acc = acc + es[l]
    return acc / tot


def _hgrn_scratch(tile, nseg):
    G = HGRN_HEADS // HGRN_GROUP_HEADS
    GW = HGRN_GROUP_HEADS * HGRN_HEAD_DIM
    return [
        pltpu.VMEM((tile, D_MODEL), BF16),
        pltpu.VMEM((tile, 3 * tile), BF16),
        pltpu.VMEM((nseg, G, tile, GW), F32),
        pltpu.VMEM((1, HGRN_WIDTH), F32),
        pltpu.VMEM((HGRN_HEADS, HGRN_HEAD_DIM, HGRN_HEAD_DIM), F32),
        pltpu.VMEM((G, tile, GW), F32),
        pltpu.VMEM((G, 3 * tile, GW), BF16),
        pltpu.VMEM((G, tile, GW), BF16),
        pltpu.VMEM((G, tile, GW), BF16),
        pltpu.VMEM((G, tile, GW), BF16),
        pltpu.VMEM((HGRN_HEADS, HGRN_HEAD_DIM, tile), BF16),
        pltpu.VMEM((HGRN_HEADS, HGRN_HEAD_DIM, tile), BF16),
        pltpu.VMEM((G, tile // HGRN_CHUNK * F32_SUBLANES, GW), F32),
    ]


def _hgrn_tile(x, ng_ref, w_ref, gamma_ref, write_out,
               hn_scr, scan_scr, proj_scr, lb_scr, state_scr, r_scr, gs_scr, qt_scr, qs_scr, v_scr, ktT_scr, keT_scr,
               dn_scr, *, tile, layer, reverse, nseg, finish_head=None):
    C = HGRN_CHUNK
    HD = HGRN_HEAD_DIM
    W = HGRN_WIDTH
    P = C // 2
    SUB = F32_SUBLANES
    GW = HGRN_GROUP_HEADS * HD
    nchunks = tile // C
    SEG_Q, SEG_F, SEG_I = 0, 1, 2

    @pl.when((pl.program_id(0) == 0) & (pl.program_id(1) == 0))
    def _():
        ti = lax.broadcasted_iota(jnp.int32, (tile, tile), 0)
        si = lax.broadcasted_iota(jnp.int32, (tile, tile), 1)
        if reverse:
            coef = (si >= ti).astype(jnp.int32) - ((si % C) >= P).astype(jnp.int32)
        else:
            coef = (si <= ti).astype(jnp.int32) - ((si % C) < P).astype(jnp.int32)
        scan = jnp.where((ti // C) == (si // C), coef, 0).astype(F32).astype(BF16)
        scan_scr[...] = jnp.concatenate([scan, scan, scan], axis=1)

    @pl.when(pl.program_id(1) == 0)
    def _():
        state_scr[...] = jnp.zeros_like(state_scr)

    lb_scr[...] = _hgrn_lower_bound(gamma_ref[...], layer)
    hn_scr[...] = _rms_rows(x, ng_ref[...]).astype(BF16)

    near_row = C - 1 if reverse else 0
    far_row = 0 if reverse else C - 1
    t2 = lax.broadcasted_iota(jnp.int32, (C, C), 0)
    s2 = lax.broadcasted_iota(jnp.int32, (C, C), 1)
    cmask = (s2 >= t2) if reverse else (s2 <= t2)
    chunk_order = range(nchunks - 1, -1, -1) if reverse else range(nchunks)

    def group_heads(g):
        return [(g * HGRN_GROUP_HEADS + hh, slice(hh * HD, (hh + 1) * HD),
                 slice(g * GW + hh * HD, g * GW + (hh + 1) * HD)) for hh in range(HGRN_GROUP_HEADS)]

    def project(g):
        for s in range(nseg):
            proj_scr[s, g] = jnp.dot(hn_scr[...], w_ref[:, s * W + g * GW:s * W + (g + 1) * GW],
                                     preferred_element_type=F32)

    def gate_and_scan(g):
        for rb in range(tile // P):
            rows = slice(rb * P, (rb + 1) * P)
            for _, lc, cols in group_heads(g):
                lb = lb_scr[:, cols]
                f = lb + (1.0 - lb) * _sigmoid(proj_scr[SEG_F, g, rows, lc])
                proj_scr[SEG_F, g, rows, lc] = f
                lf = jnp.log(f)
                p1 = lf.astype(BF16)
                d1 = lf - p1.astype(F32)
                p2 = d1.astype(BF16)
                gs_scr[g, rb * P:(rb + 1) * P, lc] = p1
                gs_scr[g, tile + rb * P:tile + (rb + 1) * P, lc] = p2
                gs_scr[g, 2 * tile + rb * P:2 * tile + (rb + 1) * P, lc] = (d1 - p2.astype(F32)).astype(BF16)

        r_scr[g] = jnp.dot(scan_scr[...], gs_scr[g], preferred_element_type=F32)

    def decay_and_recur(g):
        heads = group_heads(g)
        for rb in range(tile // P):
            rows = slice(rb * P, (rb + 1) * P)
            c0 = (rb * P // C) * C
            for _, lc, _ in heads:
                e = jnp.exp(r_scr[g, rows, lc])
                near = slice(c0 + near_row, c0 + near_row + 1)
                far = slice(c0 + far_row, c0 + far_row + 1)
                e_mid = proj_scr[SEG_F, g, near, lc] / jnp.exp(r_scr[g, near, lc])
                e_far = jnp.exp(r_scr[g, far, lc])
                qt = _silu(proj_scr[SEG_Q, g, rows, lc]) * e
                kt = (1.0 - proj_scr[SEG_F, g, rows, lc]) / e
                qt_scr[g, rows, lc] = qt.astype(BF16)
                qs_scr[g, rows, lc] = (qt * e_mid).astype(BF16)
                gs_scr[g, rb * P:(rb + 1) * P, lc] = kt.astype(BF16)
                gs_scr[g, tile + rb * P:tile + (rb + 1) * P, lc] = (kt * e_far).astype(BF16)
                v_scr[g, rows, lc] = proj_scr[SEG_I, g, rows, lc].astype(BF16)
                if rb * P == c0:
                    dn_scr[g, c0 // C * SUB:(c0 // C + 1) * SUB, lc] = jnp.broadcast_to(e_mid * e_far, (SUB, HD))

        for c in range(nchunks):
            crows = slice(c * C, (c + 1) * C)
            for h, lc, _ in heads:
                ktT_scr[h, :, crows] = gs_scr[g, c * C:(c + 1) * C, lc].T
                keT_scr[h, :, crows] = gs_scr[g, tile + c * C:tile + (c + 1) * C, lc].T

        for c in chunk_order:
            crows = slice(c * C, (c + 1) * C)
            for h, lc, cols in heads:
                st = state_scr[h]
                v = v_scr[g, crows, lc]
                a = jnp.dot(qt_scr[g, crows, lc], ktT_scr[h, :, crows], preferred_element_type=F32)
                a = jnp.where(cmask, a, 0.0).astype(BF16)
                write_out(g, lc, cols, crows,
                          jnp.dot(jnp.concatenate([a, qs_scr[g, crows, lc]], axis=1),
                                  jnp.concatenate([v, st.astype(BF16)], axis=0), preferred_element_type=F32))
                decay = jnp.broadcast_to(dn_scr[g, c * SUB:c * SUB + 1, lc], (HD, HD)).T
                state_scr[h] = st * decay + jnp.dot(keT_scr[h, :, crows], v, preferred_element_type=F32)

        if finish_head is not None:
            for _, lc, cols in heads:
                finish_head(g, lc, cols)

    ngroups = HGRN_HEADS // HGRN_GROUP_HEADS
    for k in range(ngroups + 2):
        if k < ngroups:
            project(k)
        if 0 <= k - 1 < ngroups:
            gate_and_scan(k - 1)
        if 0 <= k - 2 < ngroups:
            decay_and_recur(k - 2)


def _odd_bwd_kernel(x_ref, ng_ref, w_ref, gamma_ref, ob_ref, *scr, tile, layer):
    def write_out(g, lc, cols, rows, o):
        ob_ref[0, rows, cols] = o

    _hgrn_tile(x_ref[0], ng_ref, w_ref, gamma_ref, write_out, *scr, tile=tile, layer=layer, reverse=True, nseg=3)


def _odd_fwd_kernel(x_ref, ob_ref, ng_ref, w_ref, gamma_ref, hng_ref, wout_ref, fg_ref, o_ref,
                    of_scr, y_scr, *scr, tile, layer):
    proj_scr = scr[2]
    SEG_Z = 3

    def write_out(g, lc, cols, rows, o):
        of_scr[g, rows, lc] = o

    def finish_head(g, lc, cols):
        o = of_scr[g, :, lc] + ob_ref[0, :, cols]
        o = o * lax.rsqrt(jnp.mean(o * o, axis=-1, keepdims=True) + EPS) * hng_ref[:, cols]
        y_scr[:, cols] = (o * _silu(proj_scr[SEG_Z, g, :, lc])).astype(BF16)

    x = x_ref[0]
    _hgrn_tile(x, ng_ref, w_ref, gamma_ref, write_out, *scr, tile=tile, layer=layer, reverse=False, nseg=4,
               finish_head=finish_head)
    x2 = x + jnp.dot(y_scr[...], wout_ref[...], preferred_element_type=F32)
    o_ref[0] = _rms_rows(x2, fg_ref[...])


def _odd_layer_and_final_norm(x, norm_g, w_in, gamma_f, gamma_b, head_norm_g, w_out, final_g, layer):
    B, S, D = x.shape
    tile = min(ODD_TILE, S)
    assert S % tile == 0 and tile % HGRN_CHUNK == 0
    nt = S // tile
    W = HGRN_WIDTH
    w_bf = w_in.astype(BF16)
    wq, wff, wfb, wi, wz = (w_bf[:, n * W:(n + 1) * W] for n in range(5))
    w_bwd = jnp.concatenate([wq, wfb, wi], axis=1)
    w_fwd = jnp.concatenate([wq, wff, wi, wz], axis=1)
    ng = norm_g.reshape(1, D).astype(F32)
    params = pltpu.CompilerParams(dimension_semantics=("arbitrary", "arbitrary"),
                                  vmem_limit_bytes=V7X_VMEM_LIMIT_BYTES)

    o_b = pl.pallas_call(
        functools.partial(_odd_bwd_kernel, tile=tile, layer=layer),
        grid=(B, nt),
        in_specs=[
            pl.BlockSpec((1, tile, D), lambda b, t: (b, nt - 1 - t, 0)),
            _const_spec((1, D)),
            _const_spec((D, 3 * W)),
            _const_spec((DEPTH, W)),
        ],
        out_specs=pl.BlockSpec((1, tile, W), lambda b, t: (b, nt - 1 - t, 0)),
        out_shape=jax.ShapeDtypeStruct((B, S, W), F32),
        scratch_shapes=_hgrn_scratch(tile, 3),
        compiler_params=params,
        name="odd_layer_bwd_pass",
    )(x, ng, w_bwd, gamma_b.astype(F32))

    return pl.pallas_call(
        functools.partial(_odd_fwd_kernel, tile=tile, layer=layer),
        grid=(B, nt),
        in_specs=[
            pl.BlockSpec((1, tile, D), lambda b, t: (b, t, 0)),
            pl.BlockSpec((1, tile, W), lambda b, t: (b, t, 0)),
            _const_spec((1, D)),
            _const_spec((D, 4 * W)),
            _const_spec((DEPTH, W)),
            _const_spec((1, W)),
            _const_spec((W, D)),
            _const_spec((1, D)),
        ],
        out_specs=pl.BlockSpec((1, tile, D), lambda b, t: (b, t, 0)),
        out_shape=jax.ShapeDtypeStruct((B, S, D), x.dtype),
        scratch_shapes=[
            pltpu.VMEM((HGRN_HEADS // HGRN_GROUP_HEADS, tile, HGRN_GROUP_HEADS * HGRN_HEAD_DIM), F32),
            pltpu.VMEM((tile, W), BF16),
        ] + _hgrn_scratch(tile, 4),
        compiler_params=params,
        name="odd_layer_fwd_pass",
    )(x, o_b, ng, w_fwd, gamma_f.astype(F32), head_norm_g.reshape(1, W).astype(F32),
      w_out.astype(BF16), final_g.reshape(1, D).astype(F32))


def kernel(x, norm_g_even, w_in_even, gmlp_ln_g, gmlp_ln_b, gmlp_w_s, gmlp_b_s, attn_sink, w_out_even, norm_g_odd, w_in_odd, hgrn_gamma_fwd, hgrn_gamma_bwd, hgrn_head_norm_g, w_out_odd, final_norm_g):
    assert DEPTH == 2
    x = _even_layer(x, norm_g_even[0], w_in_even[0], gmlp_ln_g[0], gmlp_ln_b[0], gmlp_w_s[0], gmlp_b_s[0],
                    attn_sink[0], w_out_even[0])
    return _odd_layer_and_final_norm(x, norm_g_odd[0], w_in_odd[0], hgrn_gamma_fwd, hgrn_gamma_bwd,
                                     hgrn_head_norm_g[0], w_out_odd[0], final_norm_g, layer=1)
```

```python
import functools

import jax
import jax.numpy as jnp
import numpy as np
from jax import lax
from jax.experimental import pallas as pl
from jax.experimental.pallas import tpu as pltpu

F32 = jnp.float32
BF16 = jnp.bfloat16

D_MODEL = 1024
DEPTH = 2
MIX_WIDTH = 2 * D_MODEL
GMLP_WIDTH = MIX_WIDTH // 2
GMLP_GROUPS = 4
GMLP_GROUP_DIM = GMLP_WIDTH // GMLP_GROUPS
GMLP_CHUNK = 128
ATTN_WIDTH = MIX_WIDTH // 2
HEAD_DIM = 128
N_Q_HEADS = ATTN_WIDTH // HEAD_DIM
N_KV_HEADS = 2
GQA_GROUP = N_Q_HEADS // N_KV_HEADS
KV_WIDTH = N_KV_HEADS * HEAD_DIM
WINDOW = 128
ATTN_BLOCK = 128
HGRN_WIDTH = MIX_WIDTH
HGRN_HEAD_DIM = 128
HGRN_HEADS = HGRN_WIDTH // HGRN_HEAD_DIM
HGRN_CHUNK = 128
HGRN_GROUP_HEADS = 2
EPS = 1e-6
IN_EVEN = 3 * GMLP_WIDTH + ATTN_WIDTH + 2 * KV_WIDTH + ATTN_WIDTH

_OFF_UVZ = 0
_OFF_Q = 3 * GMLP_WIDTH
_OFF_KV = _OFF_Q + ATTN_WIDTH
_OFF_ZB = _OFF_KV + 2 * KV_WIDTH

F32_SUBLANES = 8

V7X_VMEM_LIMIT_BYTES = 58 * 1024 * 1024

EVEN_TILE = 512
ODD_TILE = 256

_NT = (((1,), (1,)), ((), ()))
_TN = (((0,), (0,)), ((), ()))


def _const_spec(shape):
    nd = len(shape)
    return pl.BlockSpec(shape, lambda *_: (0,) * nd, pipeline_mode=pl.Buffered(1))


def _rms_rows(x, g):
    return x * lax.rsqrt(jnp.mean(x * x, axis=-1, keepdims=True) + EPS) * g


def _sigmoid(z):
    return 0.5 * jnp.tanh(0.5 * z) + 0.5


def _silu(z):
    return z * _sigmoid(z)


def _even_kernel(x_ref, xp_ref, xn_ref, ng_ref, win_ref, lng_ref, lnb_ref, ws_ref, bs_ref,
                 ab_ref, sink_ref, wout_ref, o_ref,
                 hn_scr, uvz_scr, q_scr, kv_scr, zb_scr, y_scr, *, tile, seq):
    t = pl.program_id(1)
    ng = ng_ref[...]
    hn_scr[0:ATTN_BLOCK, :] = _rms_rows(xp_ref[0], ng).astype(BF16)
    hn_scr[ATTN_BLOCK:ATTN_BLOCK + tile, :] = _rms_rows(x_ref[0], ng).astype(BF16)
    hn_scr[ATTN_BLOCK + tile:, :] = _rms_rows(xn_ref[0], ng).astype(BF16)

    hn_cur = hn_scr[ATTN_BLOCK:ATTN_BLOCK + tile, :]
    blocks = [slice(j * ATTN_BLOCK, (j + 1) * ATTN_BLOCK) for j in range(tile // ATTN_BLOCK)]
    uvz_scr[...] = jnp.dot(hn_cur, win_ref[:, _OFF_UVZ:_OFF_Q], preferred_element_type=F32)
    q_scr[...] = (jnp.dot(hn_cur, win_ref[:, _OFF_Q:_OFF_KV], preferred_element_type=F32)
                  * (HEAD_DIM ** -0.5)).astype(BF16)

    lng = lng_ref[...]
    lnb = lnb_ref[...]
    for rows in blocks:
        v = uvz_scr[rows, GMLP_WIDTH:2 * GMLP_WIDTH]
        mu = jnp.mean(v, axis=-1, keepdims=True)
        vc = v - mu
        var = jnp.mean(vc * vc, axis=-1, keepdims=True)
        vn = (vc * lax.rsqrt(var + EPS) * lng + lnb).astype(BF16)
        mixed = jnp.concatenate(
            [jnp.dot(ws_ref[g], vn[:, g * GMLP_GROUP_DIM:(g + 1) * GMLP_GROUP_DIM], preferred_element_type=F32)
             for g in range(GMLP_GROUPS)], axis=1) + bs_ref[...]
        u = uvz_scr[rows, 0:GMLP_WIDTH]
        za = uvz_scr[rows, 2 * GMLP_WIDTH:3 * GMLP_WIDTH]
        y_scr[rows, 0:GMLP_WIDTH] = (u * mixed * _silu(za)).astype(BF16)

    kv_scr[...] = jnp.dot(hn_scr[...], win_ref[:, _OFF_KV:_OFF_ZB], preferred_element_type=F32).astype(BF16)
    zb_scr[...] = jnp.dot(hn_cur, win_ref[:, _OFF_ZB:], preferred_element_type=F32)

    key_row = lax.broadcasted_iota(jnp.int32, (3 * ATTN_BLOCK, GQA_GROUP * ATTN_BLOCK), 0)

    def scores(rows, hk):
        win = slice(rows.start, rows.start + 3 * ATTN_BLOCK)
        qs = jnp.concatenate(
            [q_scr[rows, (hk * GQA_GROUP + g) * HEAD_DIM:(hk * GQA_GROUP + g + 1) * HEAD_DIM]
             for g in range(GQA_GROUP)], axis=0)
        kw = kv_scr[win, hk * HEAD_DIM:(hk + 1) * HEAD_DIM]
        return lax.dot_general(kw, qs, _NT, preferred_element_type=F32)

    def attend(rows, hk, s):
        win = slice(rows.start, rows.start + 3 * ATTN_BLOCK)
        kpos = key_row + (t * tile + rows.start - ATTN_BLOCK)
        s = jnp.where((kpos >= 0) & (kpos < seq), s + ab_ref[hk], -jnp.inf)
        sink = sink_ref[hk]
        m = jnp.maximum(jnp.max(s, axis=0, keepdims=True), sink)
        p = jnp.exp(s - m)
        den = jnp.sum(p, axis=0, keepdims=True) + jnp.exp(sink - m)
        vw = kv_scr[win, KV_WIDTH + hk * HEAD_DIM:KV_WIDTH + (hk + 1) * HEAD_DIM]
        o = lax.dot_general(vw, p.astype(BF16), _TN, preferred_element_type=F32) / den
        for g in range(GQA_GROUP):
            c0 = (hk * GQA_GROUP + g) * HEAD_DIM
            zb = zb_scr[rows, c0:c0 + HEAD_DIM]
            y_scr[rows, GMLP_WIDTH + c0:GMLP_WIDTH + c0 + HEAD_DIM] = (
                o[:, g * ATTN_BLOCK:(g + 1) * ATTN_BLOCK].T * _silu(zb)).astype(BF16)

    pending = None
    for rows in blocks:
        for hk in range(N_KV_HEADS):
            s = scores(rows, hk)
            if pending is not None:
                attend(*pending)
            pending = (rows, hk, s)
    attend(*pending)

    o_ref[0] = x_ref[0] + jnp.dot(y_scr[...], wout_ref[...], preferred_element_type=F32)


def _even_layer(x, norm_g, w_in, ln_g, ln_b, w_s, b_s, sink, w_out):
    B, S, D = x.shape
    tile = min(EVEN_TILE, S)
    assert S % tile == 0 and tile % ATTN_BLOCK == 0
    nb = S // ATTN_BLOCK
    tpb = tile // ATTN_BLOCK

    qi = np.arange(ATTN_BLOCK)
    kj = np.arange(3 * ATTN_BLOCK) - ATTN_BLOCK
    dist = np.abs(kj[None, :] - qi[:, None]).astype(np.float32)
    slopes = np.exp2(-8.0 * np.arange(1, N_Q_HEADS + 1, dtype=np.float32) / N_Q_HEADS)
    ab = np.where(dist[None] <= WINDOW, -slopes[:, None, None] * dist[None], -np.inf).astype(np.float32)
    ab = jnp.asarray(ab.reshape(N_KV_HEADS, GQA_GROUP * ATTN_BLOCK, 3 * ATTN_BLOCK).transpose(0, 2, 1))
    sink_row = jnp.broadcast_to(sink.astype(F32).reshape(N_KV_HEADS, GQA_GROUP, 1, 1),
                                (N_KV_HEADS, GQA_GROUP, 1, ATTN_BLOCK)).reshape(N_KV_HEADS, 1, GQA_GROUP * ATTN_BLOCK)
    bs_full = jnp.repeat(b_s.astype(F32).T, GMLP_GROUP_DIM, axis=1)

    kern = functools.partial(_even_kernel, tile=tile, seq=S)
    return pl.pallas_call(
        kern,
        grid=(B, S // tile),
        in_specs=[
            pl.BlockSpec((1, tile, D), lambda b, t: (b, t, 0)),
            pl.BlockSpec((1, ATTN_BLOCK, D), lambda b, t: (b, jnp.maximum(t * tpb - 1, 0), 0)),
            pl.BlockSpec((1, ATTN_BLOCK, D), lambda b, t: (b, jnp.minimum((t + 1) * tpb, nb - 1), 0)),
            _const_spec((1, D)),
            _const_spec((D, IN_EVEN)),
            _const_spec((1, GMLP_WIDTH)),
            _const_spec((1, GMLP_WIDTH)),
            _const_spec((GMLP_GROUPS, GMLP_CHUNK, GMLP_CHUNK)),
            _const_spec((GMLP_CHUNK, GMLP_WIDTH)),
            _const_spec((N_KV_HEADS, 3 * ATTN_BLOCK, GQA_GROUP * ATTN_BLOCK)),
            _const_spec((N_KV_HEADS, 1, GQA_GROUP * ATTN_BLOCK)),
            _const_spec((GMLP_WIDTH + ATTN_WIDTH, D)),
        ],
        out_specs=pl.BlockSpec((1, tile, D), lambda b, t: (b, t, 0)),
        out_shape=jax.ShapeDtypeStruct((B, S, D), x.dtype),
        scratch_shapes=[
            pltpu.VMEM((tile + 2 * ATTN_BLOCK, D), BF16),
            pltpu.VMEM((tile, 3 * GMLP_WIDTH), F32),
            pltpu.VMEM((tile, ATTN_WIDTH), BF16),
            pltpu.VMEM((tile + 2 * ATTN_BLOCK, 2 * KV_WIDTH), BF16),
            pltpu.VMEM((tile, ATTN_WIDTH), F32),
            pltpu.VMEM((tile, GMLP_WIDTH + ATTN_WIDTH), BF16),
        ],
        compiler_params=pltpu.CompilerParams(
            dimension_semantics=("arbitrary", "arbitrary"),
            vmem_limit_bytes=V7X_VMEM_LIMIT_BYTES),
        name="even_layer",
    )(x, x, x, norm_g.reshape(1, D).astype(F32), w_in.astype(BF16),
      ln_g.reshape(1, -1).astype(F32), ln_b.reshape(1, -1).astype(F32), w_s.astype(BF16), bs_full,
      ab, sink_row, w_out.astype(BF16))


def _hgrn_lower_bound(gamma, layer):
    rows = [gamma[l:l + 1, :] for l in range(DEPTH)]
    mx = functools.reduce(jnp.maximum, rows)
    es = [jnp.exp(r - mx) for r in rows]
    tot = functools.reduce(lambda a, b: a + b, es)
    acc = jnp.zeros_like(tot)
    for l in range(1, layer + 1):
        acc = acc + es[l]
    return acc / tot


def _hgrn_scratch(tile, nseg):
    G = HGRN_HEADS // HGRN_GROUP_HEADS
    GW = HGRN_GROUP_HEADS * HGRN_HEAD_DIM
    return [
        pltpu.VMEM((tile, D_MODEL), BF16),
        pltpu.VMEM((tile, 2 * tile), BF16),
        pltpu.VMEM((nseg, G, tile, GW), F32),
        pltpu.VMEM((1, HGRN_WIDTH), F32),
        pltpu.VMEM((HGRN_HEADS, HGRN_HEAD_DIM, HGRN_HEAD_DIM), F32),
        pltpu.VMEM((G, tile, GW), F32),
        pltpu.VMEM((G, 2 * tile, GW), BF16),
        pltpu.VMEM((G, tile, GW), BF16),
        pltpu.VMEM((G, tile, GW), BF16),
        pltpu.VMEM((G, tile, GW), BF16),
        pltpu.VMEM((HGRN_HEADS, HGRN_HEAD_DIM, tile), BF16),
        pltpu.VMEM((HGRN_HEADS, HGRN_HEAD_DIM, tile), BF16),
        pltpu.VMEM((G, tile // HGRN_CHUNK * F32_SUBLANES, GW), F32),
    ]


def _hgrn_tile(x, ng_ref, w_ref, gamma_ref, write_out,
               hn_scr, scan_scr, proj_scr, lb_scr, state_scr, r_scr, gs_scr, qt_scr, qs_scr, v_scr, ktT_scr, keT_scr,
               dn_scr, *, tile, layer, reverse, nseg, seg_f, read_q, read_i, after_project=None, finish_head=None):
    C = HGRN_CHUNK
    HD = HGRN_HEAD_DIM
    W = HGRN_WIDTH
    P = C // 2
    SUB = F32_SUBLANES
    GW = HGRN_GROUP_HEADS * HD
    nchunks = tile // C

    @pl.when((pl.program_id(0) == 0) & (pl.program_id(1) == 0))
    def _():
        ti = lax.broadcasted_iota(jnp.int32, (tile, tile), 0)
        si = lax.broadcasted_iota(jnp.int32, (tile, tile), 1)
        if reverse:
            coef = (si >= ti).astype(jnp.int32) - ((si % C) >= P).astype(jnp.int32)
        else:
            coef = (si <= ti).astype(jnp.int32) - ((si % C) < P).astype(jnp.int32)
        scan = jnp.where((ti // C) == (si // C), coef, 0).astype(F32).astype(BF16)
        scan_scr[...] = jnp.concatenate([scan, scan], axis=1)

    @pl.when(pl.program_id(1) == 0)
    def _():
        state_scr[...] = jnp.zeros_like(state_scr)

    lb_scr[...] = _hgrn_lower_bound(gamma_ref[...], layer)
    hn_scr[...] = _rms_rows(x, ng_ref[...]).astype(BF16)

    near_row = C - 1 if reverse else 0
    far_row = 0 if reverse else C - 1
    t2 = lax.broadcasted_iota(jnp.int32, (C, C), 0)
    s2 = lax.broadcasted_iota(jnp.int32, (C, C), 1)
    cmask = (s2 >= t2) if reverse else (s2 <= t2)
    chunk_order = range(nchunks - 1, -1, -1) if reverse else range(nchunks)

    def group_heads(g):
        return [(g * HGRN_GROUP_HEADS + hh, slice(hh * HD, (hh + 1) * HD),
                 slice(g * GW + hh * HD, g * GW + (hh + 1) * HD)) for hh in range(HGRN_GROUP_HEADS)]

    def project(g):
        for s in range(nseg):
            proj_scr[s, g] = jnp.dot(hn_scr[...], w_ref[:, s * W + g * GW:s * W + (g + 1) * GW],
                                     preferred_element_type=F32)
        if after_project is not None:
            after_project(g, slice(g * GW, (g + 1) * GW))

    def gate_and_scan(g):
        for rb in range(tile // P):
            rows = slice(rb * P, (rb + 1) * P)
            for _, lc, cols in group_heads(g):
                lb = lb_scr[:, cols]
                f = lb + (1.0 - lb) * _sigmoid(proj_scr[seg_f, g, rows, lc])
                proj_scr[seg_f, g, rows, lc] = f
                lf = jnp.log(f)
                p1 = lf.astype(BF16)
                gs_scr[g, rb * P:(rb + 1) * P, lc] = p1
                gs_scr[g, tile + rb * P:tile + (rb + 1) * P, lc] = (lf - p1.astype(F32)).astype(BF16)

        r_scr[g] = jnp.dot(scan_scr[...], gs_scr[g], preferred_element_type=F32)

    def decay_and_recur(g):
        heads = group_heads(g)
        for rb in range(tile // P):
            rows = slice(rb * P, (rb + 1) * P)
            c0 = (rb * P // C) * C
            for _, lc, cols in heads:
                e = jnp.exp(r_scr[g, rows, lc])
                near = slice(c0 + near_row, c0 + near_row + 1)
                far = slice(c0 + far_row, c0 + far_row + 1)
                e_mid = proj_scr[seg_f, g, near, lc] / jnp.exp(r_scr[g, near, lc])
                e_far = jnp.exp(r_scr[g, far, lc])
                qt = _silu(read_q(g, rows, lc, cols)) * e
                kt = (1.0 - proj_scr[seg_f, g, rows, lc]) / e
                qt_scr[g, rows, lc] = qt.astype(BF16)
                qs_scr[g, rows, lc] = (qt * e_mid).astype(BF16)
                gs_scr[g, rb * P:(rb + 1) * P, lc] = kt.astype(BF16)
                gs_scr[g, tile + rb * P:tile + (rb + 1) * P, lc] = (kt * e_far).astype(BF16)
                v_scr[g, rows, lc] = read_i(g, rows, lc, cols)
                if rb * P == c0:
                    dn_scr[g, c0 // C * SUB:(c0 // C + 1) * SUB, lc] = jnp.broadcast_to(e_mid * e_far, (SUB, HD))

        for c in range(nchunks):
            crows = slice(c * C, (c + 1) * C)
            for h, lc, _ in heads:
                ktT_scr[h, :, crows] = gs_scr[g, c * C:(c + 1) * C, lc].T
                keT_scr[h, :, crows] = gs_scr[g, tile + c * C:tile + (c + 1) * C, lc].T

        for c in chunk_order:
            crows = slice(c * C, (c + 1) * C)
            for h, lc, cols in heads:
                st = state_scr[h]
                v = v_scr[g, crows, lc]
                a = jnp.dot(qt_scr[g, crows, lc], ktT_scr[h, :, crows], preferred_element_type=F32)
                a = jnp.where(cmask, a, 0.0).astype(BF16)
                write_out(g, lc, cols, crows,
                          jnp.dot(jnp.concatenate([a, qs_scr[g, crows, lc]], axis=1),
                                  jnp.concatenate([v, st.astype(BF16)], axis=0), preferred_element_type=F32))
                decay = jnp.broadcast_to(dn_scr[g, c * SUB:c * SUB + 1, lc], (HD, HD)).T
                state_scr[h] = st * decay + jnp.dot(keT_scr[h, :, crows], v, preferred_element_type=F32)

        if finish_head is not None:
            for _, lc, cols in heads:
                finish_head(g, lc, cols)

    ngroups = HGRN_HEADS // HGRN_GROUP_HEADS
    for k in range(ngroups + 2):
        if k < ngroups:
            project(k)
        if 0 <= k - 1 < ngroups:
            gate_and_scan(k - 1)
        if 0 <= k - 2 < ngroups:
            decay_and_recur(k - 2)


def _odd_bwd_kernel(x_ref, ng_ref, w_ref, gamma_ref, ob_ref, qx_ref, ix_ref, *scr, tile, layer):
    proj_scr = scr[2]
    SEG_Q, SEG_F, SEG_I = 0, 1, 2

    def export_q_i(g, gcols):
        qx_ref[0, :, gcols] = proj_scr[SEG_Q, g]
        ix_ref[0, :, gcols] = proj_scr[SEG_I, g].astype(BF16)

    def write_out(g, lc, cols, rows, o):
        ob_ref[0, rows, cols] = o

    _hgrn_tile(x_ref[0], ng_ref, w_ref, gamma_ref, write_out, *scr, tile=tile, layer=layer, reverse=True,
               nseg=3, seg_f=SEG_F, after_project=export_q_i,
               read_q=lambda g, rows, lc, cols: proj_scr[SEG_Q, g, rows, lc],
               read_i=lambda g, rows, lc, cols: proj_scr[SEG_I, g, rows, lc].astype(BF16))


def _odd_fwd_kernel(x_ref, ob_ref, q_ref, i_ref, ng_ref, w_ref, gamma_ref, hng_ref, wout_ref, fg_ref, o_ref,
                    of_scr, y_scr, *scr, tile, layer):
    proj_scr = scr[2]
    SEG_F, SEG_Z = 0, 1

    def write_out(g, lc, cols, rows, o):
        of_scr[g, rows, lc] = o

    def finish_head(g, lc, cols):
        o = of_scr[g, :, lc] + ob_ref[0, :, cols]
        o = o * lax.rsqrt(jnp.mean(o * o, axis=-1, keepdims=True) + EPS) * hng_ref[:, cols]
        y_scr[:, cols] = (o * _silu(proj_scr[SEG_Z, g, :, lc])).astype(BF16)

    x = x_ref[0]
    _hgrn_tile(x, ng_ref, w_ref, gamma_ref, write_out, *scr, tile=tile, layer=layer, reverse=False,
               nseg=2, seg_f=SEG_F, finish_head=finish_head,
               read_q=lambda g, rows, lc, cols: q_ref[0, rows, cols],
               read_i=lambda g, rows, lc, cols: i_ref[0, rows, cols])
    x2 = x + jnp.dot(y_scr[...], wout_ref[...], preferred_element_type=F32)
    o_ref[0] = _rms_rows(x2, fg_ref[...])


def _odd_layer_and_final_norm(x, norm_g, w_in, gamma_f, gamma_b, head_norm_g, w_out, final_g, layer):
    B, S, D = x.shape
    tile = min(ODD_TILE, S)
    assert S % tile == 0 and tile % HGRN_CHUNK == 0
    nt = S // tile
    W = HGRN_WIDTH
    w_bf = w_in.astype(BF16)
    wq, wff, wfb, wi, wz = (w_bf[:, n * W:(n + 1) * W] for n in range(5))
    w_bwd = jnp.concatenate([wq, wfb, wi], axis=1)
    w_fwd = jnp.concatenate([wff, wz], axis=1)
    ng = norm_g.reshape(1, D).astype(F32)
    params = pltpu.CompilerParams(dimension_semantics=("arbitrary", "arbitrary"),
                                  vmem_limit_bytes=V7X_VMEM_LIMIT_BYTES)

    seq_rev = pl.BlockSpec((1, tile, W), lambda b, t: (b, nt - 1 - t, 0))
    seq_fwd = pl.BlockSpec((1, tile, W), lambda b, t: (b, t, 0))
    o_b, q_proj, i_proj = pl.pallas_call(
        functools.partial(_odd_bwd_kernel, tile=tile, layer=layer),
        grid=(B, nt),
        in_specs=[
            pl.BlockSpec((1, tile, D), lambda b, t: (b, nt - 1 - t, 0)),
            _const_spec((1, D)),
            _const_spec((D, 3 * W)),
            _const_spec((DEPTH, W)),
        ],
        out_specs=[seq_rev, seq_rev, seq_rev],
        out_shape=[jax.ShapeDtypeStruct((B, S, W), F32), jax.ShapeDtypeStruct((B, S, W), F32),
                   jax.ShapeDtypeStruct((B, S, W), BF16)],
        scratch_shapes=_hgrn_scratch(tile, 3),
        compiler_params=params,
        name="odd_layer_bwd_pass",
    )(x, ng, w_bwd, gamma_b.astype(F32))

    return pl.pallas_call(
        functools.partial(_odd_fwd_kernel, tile=tile, layer=layer),
        grid=(B, nt),
        in_specs=[
            pl.BlockSpec((1, tile, D), lambda b, t: (b, t, 0)),
            seq_fwd, seq_fwd, seq_fwd,
            _const_spec((1, D)),
            _const_spec((D, 2 * W)),
            _const_spec((DEPTH, W)),
            _const_spec((1, W)),
            _const_spec((W, D)),
            _const_spec((1, D)),
        ],
        out_specs=pl.BlockSpec((1, tile, D), lambda b, t: (b, t, 0)),
        out_shape=jax.ShapeDtypeStruct((B, S, D), x.dtype),
        scratch_shapes=[
            pltpu.VMEM((HGRN_HEADS // HGRN_GROUP_HEADS, tile, HGRN_GROUP_HEADS * HGRN_HEAD_DIM), F32),
            pltpu.VMEM((tile, W), BF16),
        ] + _hgrn_scratch(tile, 2),
        compiler_params=params,
        name="odd_layer_fwd_pass",
    )(x, o_b, q_proj, i_proj, ng, w_fwd, gamma_f.astype(F32), head_norm_g.reshape(1, W).astype(F32),
      w_out.astype(BF16), final_g.reshape(1, D).astype(F32))


def kernel(x, norm_g_even, w_in_even, gmlp_ln_g, gmlp_ln_b, gmlp_w_s, gmlp_b_s, attn_sink, w_out_even, norm_g_odd, w_in_odd, hgrn_gamma_fwd, hgrn_gamma_bwd, hgrn_head_norm_g, w_out_odd, final_norm_g):
    assert DEPTH == 2
    x = _even_layer(x, norm_g_even[0], w_in_even[0], gmlp_ln_g[0], gmlp_ln_b[0], gmlp_w_s[0], gmlp_b_s[0],
                    attn_sink[0], w_out_even[0])
    return _odd_layer_and_final_norm(x, norm_g_odd[0], w_in_odd[0], hgrn_gamma_fwd, hgrn_gamma_bwd,
                                     hgrn_head_norm_g[0], w_out_odd[0], final_norm_g, layer=1)
```

```python
import functools

import jax
import jax.numpy as jnp
import numpy as np
from jax import lax
from jax.experimental import pallas as pl
from jax.experimental.pallas import tpu as pltpu

F32 = jnp.float32
BF16 = jnp.bfloat16

D_MODEL = 1024
DEPTH = 2
MIX_WIDTH = 2 * D_MODEL
GMLP_WIDTH = MIX_WIDTH // 2
GMLP_GROUPS = 4
GMLP_GROUP_DIM = GMLP_WIDTH // GMLP_GROUPS
GMLP_CHUNK = 128
ATTN_WIDTH = MIX_WIDTH // 2
HEAD_DIM = 128
N_Q_HEADS = ATTN_WIDTH // HEAD_DIM
N_KV_HEADS = 2
GQA_GROUP = N_Q_HEADS // N_KV_HEADS
KV_WIDTH = N_KV_HEADS * HEAD_DIM
WINDOW = 128
ATTN_BLOCK = 128
HGRN_WIDTH = MIX_WIDTH
HGRN_HEAD_DIM = 128
HGRN_HEADS = HGRN_WIDTH // HGRN_HEAD_DIM
HGRN_CHUNK = 128
HGRN_GROUP_HEADS = 4
EPS = 1e-6
IN_EVEN = 3 * GMLP_WIDTH + ATTN_WIDTH + 2 * KV_WIDTH + ATTN_WIDTH

_OFF_UVZ = 0
_OFF_Q = 3 * GMLP_WIDTH
_OFF_KV = _OFF_Q + ATTN_WIDTH
_OFF_ZB = _OFF_KV + 2 * KV_WIDTH

F32_SUBLANES = 8

V7X_VMEM_LIMIT_BYTES = 58 * 1024 * 1024

EVEN_TILE = 512
ODD_TILE = 256

_NT = (((1,), (1,)), ((), ()))
_TN = (((0,), (0,)), ((), ()))


def _const_spec(shape):
    nd = len(shape)
    return pl.BlockSpec(shape, lambda *_: (0,) * nd, pipeline_mode=pl.Buffered(1))


def _rms_rows(x, g):
    return x * lax.rsqrt(jnp.mean(x * x, axis=-1, keepdims=True) + EPS) * g


def _sigmoid(z):
    return 0.5 * jnp.tanh(0.5 * z) + 0.5


def _silu(z):
    return z * _sigmoid(z)


def _even_kernel(x_ref, xp_ref, xn_ref, ng_ref, win_ref, lng_ref, lnb_ref, ws_ref, bs_ref,
                 ab_ref, sink_ref, wout_ref, o_ref,
                 hn_scr, uvz_scr, q_scr, kv_scr, zb_scr, y_scr, *, tile, seq):
    t = pl.program_id(1)
    ng = ng_ref[...]
    hn_scr[0:ATTN_BLOCK, :] = _rms_rows(xp_ref[0], ng).astype(BF16)
    hn_scr[ATTN_BLOCK:ATTN_BLOCK + tile, :] = _rms_rows(x_ref[0], ng).astype(BF16)
    hn_scr[ATTN_BLOCK + tile:, :] = _rms_rows(xn_ref[0], ng).astype(BF16)

    hn_cur = hn_scr[ATTN_BLOCK:ATTN_BLOCK + tile, :]
    blocks = [slice(j * ATTN_BLOCK, (j + 1) * ATTN_BLOCK) for j in range(tile // ATTN_BLOCK)]
    uvz_scr[...] = jnp.dot(hn_cur, win_ref[:, _OFF_UVZ:_OFF_Q], preferred_element_type=F32)
    q_scr[...] = (jnp.dot(hn_cur, win_ref[:, _OFF_Q:_OFF_KV], preferred_element_type=F32)
                  * (HEAD_DIM ** -0.5)).astype(BF16)

    lng = lng_ref[...]
    lnb = lnb_ref[...]
    for rows in blocks:
        v = uvz_scr[rows, GMLP_WIDTH:2 * GMLP_WIDTH]
        mu = jnp.mean(v, axis=-1, keepdims=True)
        vc = v - mu
        var = jnp.mean(vc * vc, axis=-1, keepdims=True)
        vn = (vc * lax.rsqrt(var + EPS) * lng + lnb).astype(BF16)
        mixed = jnp.concatenate(
            [jnp.dot(ws_ref[g], vn[:, g * GMLP_GROUP_DIM:(g + 1) * GMLP_GROUP_DIM], preferred_element_type=F32)
             for g in range(GMLP_GROUPS)], axis=1) + bs_ref[...]
        u = uvz_scr[rows, 0:GMLP_WIDTH]
        za = uvz_scr[rows, 2 * GMLP_WIDTH:3 * GMLP_WIDTH]
        y_scr[rows, 0:GMLP_WIDTH] = (u * mixed * _silu(za)).astype(BF16)

    kv_scr[...] = jnp.dot(hn_scr[...], win_ref[:, _OFF_KV:_OFF_ZB], preferred_element_type=F32).astype(BF16)
    zb_scr[...] = jnp.dot(hn_cur, win_ref[:, _OFF_ZB:], preferred_element_type=F32)

    key_row = lax.broadcasted_iota(jnp.int32, (3 * ATTN_BLOCK, GQA_GROUP * ATTN_BLOCK), 0)

    def scores(rows, hk):
        win = slice(rows.start, rows.start + 3 * ATTN_BLOCK)
        qs = jnp.concatenate(
            [q_scr[rows, (hk * GQA_GROUP + g) * HEAD_DIM:(hk * GQA_GROUP + g + 1) * HEAD_DIM]
             for g in range(GQA_GROUP)], axis=0)
        kw = kv_scr[win, hk * HEAD_DIM:(hk + 1) * HEAD_DIM]
        return lax.dot_general(kw, qs, _NT, preferred_element_type=F32)

    def attend(rows, hk, s):
        win = slice(rows.start, rows.start + 3 * ATTN_BLOCK)
        kpos = key_row + (t * tile + rows.start - ATTN_BLOCK)
        s = jnp.where((kpos >= 0) & (kpos < seq), s + ab_ref[hk], -jnp.inf)
        sink = sink_ref[hk]
        m = jnp.maximum(jnp.max(s, axis=0, keepdims=True), sink)
        p = jnp.exp(s - m)
        den = jnp.sum(p, axis=0, keepdims=True) + jnp.exp(sink - m)
        vw = kv_scr[win, KV_WIDTH + hk * HEAD_DIM:KV_WIDTH + (hk + 1) * HEAD_DIM]
        o = lax.dot_general(vw, p.astype(BF16), _TN, preferred_element_type=F32) / den
        for g in range(GQA_GROUP):
            c0 = (hk * GQA_GROUP + g) * HEAD_DIM
            zb = zb_scr[rows, c0:c0 + HEAD_DIM]
            y_scr[rows, GMLP_WIDTH + c0:GMLP_WIDTH + c0 + HEAD_DIM] = (
                o[:, g * ATTN_BLOCK:(g + 1) * ATTN_BLOCK].T * _silu(zb)).astype(BF16)

    pending = None
    for rows in blocks:
        for hk in range(N_KV_HEADS):
            s = scores(rows, hk)
            if pending is not None:
                attend(*pending)
            pending = (rows, hk, s)
    attend(*pending)

    o_ref[0] = x_ref[0] + jnp.dot(y_scr[...], wout_ref[...], preferred_element_type=F32)


def _even_layer(x, norm_g, w_in, ln_g, ln_b, w_s, b_s, sink, w_out):
    B, S, D = x.shape
    tile = min(EVEN_TILE, S)
    assert S % tile == 0 and tile % ATTN_BLOCK == 0
    nb = S // ATTN_BLOCK
    tpb = tile // ATTN_BLOCK

    qi = np.arange(ATTN_BLOCK)
    kj = np.arange(3 * ATTN_BLOCK) - ATTN_BLOCK
    dist = np.abs(kj[None, :] - qi[:, None]).astype(np.float32)
    slopes = np.exp2(-8.0 * np.arange(1, N_Q_HEADS + 1, dtype=np.float32) / N_Q_HEADS)
    ab = np.where(dist[None] <= WINDOW, -slopes[:, None, None] * dist[None], -np.inf).astype(np.float32)
    ab = jnp.asarray(ab.reshape(N_KV_HEADS, GQA_GROUP * ATTN_BLOCK, 3 * ATTN_BLOCK).transpose(0, 2, 1))
    sink_row = jnp.broadcast_to(sink.astype(F32).reshape(N_KV_HEADS, GQA_GROUP, 1, 1),
                                (N_KV_HEADS, GQA_GROUP, 1, ATTN_BLOCK)).reshape(N_KV_HEADS, 1, GQA_GROUP * ATTN_BLOCK)
    bs_full = jnp.repeat(b_s.astype(F32).T, GMLP_GROUP_DIM, axis=1)

    kern = functools.partial(_even_kernel, tile=tile, seq=S)
    return pl.pallas_call(
        kern,
        grid=(B, S // tile),
        in_specs=[
            pl.BlockSpec((1, tile, D), lambda b, t: (b, t, 0)),
            pl.BlockSpec((1, ATTN_BLOCK, D), lambda b, t: (b, jnp.maximum(t * tpb - 1, 0), 0)),
            pl.BlockSpec((1, ATTN_BLOCK, D), lambda b, t: (b, jnp.minimum((t + 1) * tpb, nb - 1), 0)),
            _const_spec((1, D)),
            _const_spec((D, IN_EVEN)),
            _const_spec((1, GMLP_WIDTH)),
            _const_spec((1, GMLP_WIDTH)),
            _const_spec((GMLP_GROUPS, GMLP_CHUNK, GMLP_CHUNK)),
            _const_spec((GMLP_CHUNK, GMLP_WIDTH)),
            _const_spec((N_KV_HEADS, 3 * ATTN_BLOCK, GQA_GROUP * ATTN_BLOCK)),
            _const_spec((N_KV_HEADS, 1, GQA_GROUP * ATTN_BLOCK)),
            _const_spec((GMLP_WIDTH + ATTN_WIDTH, D)),
        ],
        out_specs=pl.BlockSpec((1, tile, D), lambda b, t: (b, t, 0)),
        out_shape=jax.ShapeDtypeStruct((B, S, D), x.dtype),
        scratch_shapes=[
            pltpu.VMEM((tile + 2 * ATTN_BLOCK, D), BF16),
            pltpu.VMEM((tile, 3 * GMLP_WIDTH), F32),
            pltpu.VMEM((tile, ATTN_WIDTH), BF16),
            pltpu.VMEM((tile + 2 * ATTN_BLOCK, 2 * KV_WIDTH), BF16),
            pltpu.VMEM((tile, ATTN_WIDTH), F32),
            pltpu.VMEM((tile, GMLP_WIDTH + ATTN_WIDTH), BF16),
        ],
        compiler_params=pltpu.CompilerParams(
            dimension_semantics=("arbitrary", "arbitrary"),
            vmem_limit_bytes=V7X_VMEM_LIMIT_BYTES),
        name="even_layer",
    )(x, x, x, norm_g.reshape(1, D).astype(F32), w_in.astype(BF16),
      ln_g.reshape(1, -1).astype(F32), ln_b.reshape(1, -1).astype(F32), w_s.astype(BF16), bs_full,
      ab, sink_row, w_out.astype(BF16))


def _hgrn_lower_bound(gamma, layer):
    rows = [gamma[l:l + 1, :] for l in range(DEPTH)]
    mx = functools.reduce(jnp.maximum, rows)
    es = [jnp.exp(r - mx) for r in rows]
    tot = functools.reduce(lambda a, b: a + b, es)
    acc = jnp.zeros_like(tot)
    for l in range(1, layer + 1):
        acc = acc + es[l]
    return acc / tot


def _hgrn_scratch(tile, nseg):
    G = HGRN_HEADS // HGRN_GROUP_HEADS
    GW = HGRN_GROUP_HEADS * HGRN_HEAD_DIM
    return [
        pltpu.VMEM((tile, D_MODEL), BF16),
        pltpu.VMEM((HGRN_CHUNK, 2 * HGRN_CHUNK), BF16),
        pltpu.VMEM((nseg, G, tile, GW), F32),
        pltpu.VMEM((1, HGRN_WIDTH), F32),
        pltpu.VMEM((HGRN_HEADS, HGRN_HEAD_DIM, HGRN_HEAD_DIM), F32),
        pltpu.VMEM((G, tile, GW), F32),
        pltpu.VMEM((G, 2 * tile, GW), BF16),
        pltpu.VMEM((G, tile, GW), BF16),
        pltpu.VMEM((G, tile, GW), BF16),
        pltpu.VMEM((G, tile, GW), BF16),
        pltpu.VMEM((HGRN_HEADS, HGRN_HEAD_DIM, tile), BF16),
        pltpu.VMEM((HGRN_HEADS, HGRN_HEAD_DIM, tile), BF16),
        pltpu.VMEM((G, tile // HGRN_CHUNK * F32_SUBLANES, GW), F32),
    ]


def _hgrn_tile(x, ng_ref, w_ref, gamma_ref, write_out,
               hn_scr, scan_scr, proj_scr, lb_scr, state_scr, r_scr, gs_scr, qt_scr, qs_scr, v_scr, ktT_scr, keT_scr,
               dn_scr, *, tile, layer, reverse, nseg, seg_f, read_q, read_i, after_project=None, finish_head=None):
    C = HGRN_CHUNK
    HD = HGRN_HEAD_DIM
    W = HGRN_WIDTH
    P = C // 2
    SUB = F32_SUBLANES
    GW = HGRN_GROUP_HEADS * HD
    nchunks = tile // C

    @pl.when((pl.program_id(0) == 0) & (pl.program_id(1) == 0))
    def _():
        ti = lax.broadcasted_iota(jnp.int32, (C, C), 0)
        si = lax.broadcasted_iota(jnp.int32, (C, C), 1)
        if reverse:
            coef = (si >= ti).astype(jnp.int32) - (si >= P).astype(jnp.int32)
        else:
            coef = (si <= ti).astype(jnp.int32) - (si < P).astype(jnp.int32)
        scan = coef.astype(F32).astype(BF16)
        scan_scr[...] = jnp.concatenate([scan, scan], axis=1)

    @pl.when(pl.program_id(1) == 0)
    def _():
        state_scr[...] = jnp.zeros_like(state_scr)

    lb_scr[...] = _hgrn_lower_bound(gamma_ref[...], layer)
    hn_scr[...] = _rms_rows(x, ng_ref[...]).astype(BF16)

    near_row = C - 1 if reverse else 0
    far_row = 0 if reverse else C - 1
    t2 = lax.broadcasted_iota(jnp.int32, (C, C), 0)
    s2 = lax.broadcasted_iota(jnp.int32, (C, C), 1)
    cmask = (s2 >= t2) if reverse else (s2 <= t2)
    chunk_order = range(nchunks - 1, -1, -1) if reverse else range(nchunks)

    def piece_rows(rb, piece):
        c, r = divmod(rb * P, C)
        return slice((2 * c + piece) * C + r, (2 * c + piece) * C + r + P)

    def group_heads(g):
        return [(g * HGRN_GROUP_HEADS + hh, slice(hh * HD, (hh + 1) * HD),
                 slice(g * GW + hh * HD, g * GW + (hh + 1) * HD)) for hh in range(HGRN_GROUP_HEADS)]

    def project(g):
        for s in range(nseg):
            proj_scr[s, g] = jnp.dot(hn_scr[...], w_ref[:, s * W + g * GW:s * W + (g + 1) * GW],
                                     preferred_element_type=F32)
        if after_project is not None:
            after_project(g, slice(g * GW, (g + 1) * GW))

    def gate_and_scan(g):
        for rb in range(tile // P):
            rows = slice(rb * P, (rb + 1) * P)
            for _, lc, cols in group_heads(g):
                lb = lb_scr[:, cols]
                f = lb + (1.0 - lb) * _sigmoid(proj_scr[seg_f, g, rows, lc])
                proj_scr[seg_f, g, rows, lc] = f
                lf = jnp.log(f)
                p1 = lf.astype(BF16)
                gs_scr[g, piece_rows(rb, 0), lc] = p1
                gs_scr[g, piece_rows(rb, 1), lc] = (lf - p1.astype(F32)).astype(BF16)

        for c in range(nchunks):
            r_scr[g, c * C:(c + 1) * C] = jnp.dot(scan_scr[...], gs_scr[g, 2 * c * C:2 * (c + 1) * C],
                                                   preferred_element_type=F32)

    def decay_and_recur(g):
        heads = group_heads(g)
        for rb in range(tile // P):
            rows = slice(rb * P, (rb + 1) * P)
            c0 = (rb * P // C) * C
            for _, lc, cols in heads:
                e = jnp.exp(r_scr[g, rows, lc])
                near = slice(c0 + near_row, c0 + near_row + 1)
                far = slice(c0 + far_row, c0 + far_row + 1)
                e_mid = proj_scr[seg_f, g, near, lc] / jnp.exp(r_scr[g, near, lc])
                e_far = jnp.exp(r_scr[g, far, lc])
                qt = _silu(read_q(g, rows, lc, cols)) * e
                kt = (1.0 - proj_scr[seg_f, g, rows, lc]) / e
                qt_scr[g, rows, lc] = qt.astype(BF16)
                qs_scr[g, rows, lc] = (qt * e_mid).astype(BF16)
                gs_scr[g, piece_rows(rb, 0), lc] = kt.astype(BF16)
                gs_scr[g, piece_rows(rb, 1), lc] = (kt * e_far).astype(BF16)
                v_scr[g, rows, lc] = read_i(g, rows, lc, cols)
                if rb * P == c0:
                    dn_scr[g, c0 // C * SUB:(c0 // C + 1) * SUB, lc] = jnp.broadcast_to(e_mid * e_far, (SUB, HD))

        for c in range(nchunks):
            crows = slice(c * C, (c + 1) * C)
            for h, lc, _ in heads:
                ktT_scr[h, :, crows] = gs_scr[g, 2 * c * C:(2 * c + 1) * C, lc].T
                keT_scr[h, :, crows] = gs_scr[g, (2 * c + 1) * C:(2 * c + 2) * C, lc].T

        for c in chunk_order:
            crows = slice(c * C, (c + 1) * C)
            for h, lc, cols in heads:
                st = state_scr[h]
                v = v_scr[g, crows, lc]
                a = jnp.dot(qt_scr[g, crows, lc], ktT_scr[h, :, crows], preferred_element_type=F32)
                a = jnp.where(cmask, a, 0.0).astype(BF16)
                write_out(g, lc, cols, crows,
                          jnp.dot(jnp.concatenate([a, qs_scr[g, crows, lc]], axis=1),
                                  jnp.concatenate([v, st.astype(BF16)], axis=0), preferred_element_type=F32))
                decay = jnp.broadcast_to(dn_scr[g, c * SUB:c * SUB + 1, lc], (HD, HD)).T
                state_scr[h] = st * decay + jnp.dot(keT_scr[h, :, crows], v, preferred_element_type=F32)

        if finish_head is not None:
            for _, lc, cols in heads:
                finish_head(g, lc, cols)

    ngroups = HGRN_HEADS // HGRN_GROUP_HEADS
    for k in range(ngroups + 2):
        if k < ngroups:
            project(k)
        if 0 <= k - 1 < ngroups:
            gate_and_scan(k - 1)
        if 0 <= k - 2 < ngroups:
            decay_and_recur(k - 2)


def _odd_bwd_kernel(x_ref, ng_ref, w_ref, gamma_ref, ob_ref, qx_ref, ix_ref, *scr, tile, layer):
    proj_scr = scr[2]
    SEG_Q, SEG_F, SEG_I = 0, 1, 2

    def export_q_i(g, gcols):
        qx_ref[0, :, gcols] = proj_scr[SEG_Q, g]
        ix_ref[0, :, gcols] = proj_scr[SEG_I, g].astype(BF16)

    def write_out(g, lc, cols, rows, o):
        ob_ref[0, rows, cols] = o

    _hgrn_tile(x_ref[0], ng_ref, w_ref, gamma_ref, write_out, *scr, tile=tile, layer=layer, reverse=True,
               nseg=3, seg_f=SEG_F, after_project=export_q_i,
               read_q=lambda g, rows, lc, cols: proj_scr[SEG_Q, g, rows, lc],
               read_i=lambda g, rows, lc, cols: proj_scr[SEG_I, g, rows, lc].astype(BF16))


def _odd_fwd_kernel(x_ref, ob_ref, q_ref, i_ref, ng_ref, w_ref, gamma_ref, hng_ref, wout_ref, fg_ref, o_ref,
                    of_scr, y_scr, *scr, tile, layer):
    proj_scr = scr[2]
    SEG_F, SEG_Z = 0, 1

    def write_out(g, lc, cols, rows, o):
        of_scr[g, rows, lc] = o

    def finish_head(g, lc, cols):
        o = of_scr[g, :, lc] + ob_ref[0, :, cols]
        o = o * lax.rsqrt(jnp.mean(o * o, axis=-1, keepdims=True) + EPS) * hng_ref[:, cols]
        y_scr[:, cols] = (o * _silu(proj_scr[SEG_Z, g, :, lc])).astype(BF16)

    _hgrn_tile(x_ref[0], ng_ref, w_ref, gamma_ref, write_out, *scr, tile=tile, layer=layer, reverse=False,
               nseg=2, seg_f=SEG_F, finish_head=finish_head,
               read_q=lambda g, rows, lc, cols: q_ref[0, rows, cols],
               read_i=lambda g, rows, lc, cols: i_ref[0, rows, cols])
    x2 = x_ref[0] + jnp.dot(y_scr[...], wout_ref[...], preferred_element_type=F32)
    o_ref[0] = _rms_rows(x2, fg_ref[...])


def _odd_layer_and_final_norm(x, norm_g, w_in, gamma_f, gamma_b, head_norm_g, w_out, final_g, layer):
    B, S, D = x.shape
    tile = min(ODD_TILE, S)
    assert S % tile == 0 and tile % HGRN_CHUNK == 0
    nt = S // tile
    W = HGRN_WIDTH
    w_bf = w_in.astype(BF16)
    wq, wff, wfb, wi, wz = (w_bf[:, n * W:(n + 1) * W] for n in range(5))
    w_bwd = jnp.concatenate([wq, wfb, wi], axis=1)
    w_fwd = jnp.concatenate([wff, wz], axis=1)
    ng = norm_g.reshape(1, D).astype(F32)
    params = pltpu.CompilerParams(dimension_semantics=("arbitrary", "arbitrary"),
                                  vmem_limit_bytes=V7X_VMEM_LIMIT_BYTES)

    seq_rev = pl.BlockSpec((1, tile, W), lambda b, t: (b, nt - 1 - t, 0))
    seq_fwd = pl.BlockSpec((1, tile, W), lambda b, t: (b, t, 0))
    o_b, q_proj, i_proj = pl.pallas_call(
        functools.partial(_odd_bwd_kernel, tile=tile, layer=layer),
        grid=(B, nt),
        in_specs=[
            pl.BlockSpec((1, tile, D), lambda b, t: (b, nt - 1 - t, 0)),
            _const_spec((1, D)),
            _const_spec((D, 3 * W)),
            _const_spec((DEPTH, W)),
        ],
        out_specs=[seq_rev, seq_rev, seq_rev],
        out_shape=[jax.ShapeDtypeStruct((B, S, W), F32), jax.ShapeDtypeStruct((B, S, W), F32),
                   jax.ShapeDtypeStruct((B, S, W), BF16)],
        scratch_shapes=_hgrn_scratch(tile, 3),
        compiler_params=params,
        name="odd_layer_bwd_pass",
    )(x, ng, w_bwd, gamma_b.astype(F32))

    return pl.pallas_call(
        functools.partial(_odd_fwd_kernel, tile=tile, layer=layer),
        grid=(B, nt),
        in_specs=[
            pl.BlockSpec((1, tile, D), lambda b, t: (b, t, 0)),
            seq_fwd, seq_fwd, seq_fwd,
            _const_spec((1, D)),
            _const_spec((D, 2 * W)),
            _const_spec((DEPTH, W)),
            _const_spec((1, W)),
            _const_spec((W, D)),
            _const_spec((1, D)),
        ],
        out_specs=pl.BlockSpec((1, tile, D), lambda b, t: (b, t, 0)),
        out_shape=jax.ShapeDtypeStruct((B, S, D), x.dtype),
        scratch_shapes=[
            pltpu.VMEM((HGRN_HEADS // HGRN_GROUP_HEADS, tile, HGRN_GROUP_HEADS * HGRN_HEAD_DIM), F32),
            pltpu.VMEM((tile, W), BF16),
        ] + _hgrn_scratch(tile, 2),
        compiler_params=params,
        name="odd_layer_fwd_pass",
    )(x, o_b, q_proj, i_proj, ng, w_fwd, gamma_f.astype(F32), head_norm_g.reshape(1, W).astype(F32),
      w_out.astype(BF16), final_g.reshape(1, D).astype(F32))


def kernel(x, norm_g_even, w_in_even, gmlp_ln_g, gmlp_ln_b, gmlp_w_s, gmlp_b_s, attn_sink, w_out_even, norm_g_odd, w_in_odd, hgrn_gamma_fwd, hgrn_gamma_bwd, hgrn_head_norm_g, w_out_odd, final_norm_g):
    assert DEPTH == 2
    x = _even_layer(x, norm_g_even[0], w_in_even[0], gmlp_ln_g[0], gmlp_ln_b[0], gmlp_w_s[0], gmlp_b_s[0],
                    attn_sink[0], w_out_even[0])
    return _odd_layer_and_final_norm(x, norm_g_odd[0], w_in_odd[0], hgrn_gamma_fwd, hgrn_gamma_bwd,
                                     hgrn_head_norm_g[0], w_out_odd[0], final_norm_g, layer=1)
```

```python
import functools

import jax
import jax.numpy as jnp
import numpy as np
from jax import lax
from jax.experimental import pallas as pl
from jax.experimental.pallas import tpu as pltpu

F32 = jnp.float32
BF16 = jnp.bfloat16

D_MODEL = 1024
DEPTH = 2
MIX_WIDTH = 2 * D_MODEL
GMLP_WIDTH = MIX_WIDTH // 2
GMLP_GROUPS = 4
GMLP_GROUP_DIM = GMLP_WIDTH // GMLP_GROUPS
GMLP_CHUNK = 128
ATTN_WIDTH = MIX_WIDTH // 2
HEAD_DIM = 128
N_Q_HEADS = ATTN_WIDTH // HEAD_DIM
N_KV_HEADS = 2
GQA_GROUP = N_Q_HEADS // N_KV_HEADS
KV_WIDTH = N_KV_HEADS * HEAD_DIM
WINDOW = 128
ATTN_BLOCK = 128
HGRN_WIDTH = MIX_WIDTH
HGRN_HEAD_DIM = 128
HGRN_HEADS = HGRN_WIDTH // HGRN_HEAD_DIM
HGRN_CHUNK = 128
HGRN_GROUP_HEADS = 4
EPS = 1e-6
IN_EVEN = 3 * GMLP_WIDTH + ATTN_WIDTH + 2 * KV_WIDTH + ATTN_WIDTH

_OFF_UVZ = 0
_OFF_Q = 3 * GMLP_WIDTH
_OFF_KV = _OFF_Q + ATTN_WIDTH
_OFF_ZB = _OFF_KV + 2 * KV_WIDTH

F32_SUBLANES = 8

V7X_VMEM_LIMIT_BYTES = 58 * 1024 * 1024

EVEN_TILE = 512
ODD_TILE = 256

_NT = (((1,), (1,)), ((), ()))
_TN = (((0,), (0,)), ((), ()))


def _const_spec(shape):
    nd = len(shape)
    return pl.BlockSpec(shape, lambda *_: (0,) * nd, pipeline_mode=pl.Buffered(1))


def _rms_rows(x, g):
    return x * lax.rsqrt(jnp.mean(x * x, axis=-1, keepdims=True) + EPS) * g


def _sigmoid(z):
    return 0.5 * jnp.tanh(0.5 * z) + 0.5


def _silu(z):
    return z * _sigmoid(z)


def _even_kernel(x_ref, xp_ref, xn_ref, ng_ref, win_ref, lng_ref, lnb_ref, ws_ref, bs_ref,
                 ab_ref, sink_ref, wout_ref, o_ref,
                 hn_scr, uvz_scr, q_scr, kv_scr, zb_scr, y_scr, *, tile, seq):
    t = pl.program_id(1)
    ng = ng_ref[...]
    hn_scr[0:ATTN_BLOCK, :] = _rms_rows(xp_ref[0], ng).astype(BF16)
    hn_scr[ATTN_BLOCK:ATTN_BLOCK + tile, :] = _rms_rows(x_ref[0], ng).astype(BF16)
    hn_scr[ATTN_BLOCK + tile:, :] = _rms_rows(xn_ref[0], ng).astype(BF16)

    hn_cur = hn_scr[ATTN_BLOCK:ATTN_BLOCK + tile, :]
    blocks = [slice(j * ATTN_BLOCK, (j + 1) * ATTN_BLOCK) for j in range(tile // ATTN_BLOCK)]
    uvz_scr[...] = jnp.dot(hn_cur, win_ref[:, _OFF_UVZ:_OFF_Q], preferred_element_type=F32)
    q_scr[...] = (jnp.dot(hn_cur, win_ref[:, _OFF_Q:_OFF_KV], preferred_element_type=F32)
                  * (HEAD_DIM ** -0.5)).astype(BF16)

    lng = lng_ref[...]
    lnb = lnb_ref[...]
    for rows in blocks:
        v = uvz_scr[rows, GMLP_WIDTH:2 * GMLP_WIDTH]
        mu = jnp.mean(v, axis=-1, keepdims=True)
        vc = v - mu
        var = jnp.mean(vc * vc, axis=-1, keepdims=True)
        vn = (vc * lax.rsqrt(var + EPS) * lng + lnb).astype(BF16)
        mixed = jnp.concatenate(
            [jnp.dot(ws_ref[g], vn[:, g * GMLP_GROUP_DIM:(g + 1) * GMLP_GROUP_DIM], preferred_element_type=F32)
             for g in range(GMLP_GROUPS)], axis=1) + bs_ref[...]
        u = uvz_scr[rows, 0:GMLP_WIDTH]
        za = uvz_scr[rows, 2 * GMLP_WIDTH:3 * GMLP_WIDTH]
        y_scr[rows, 0:GMLP_WIDTH] = (u * mixed * _silu(za)).astype(BF16)

    kv_scr[...] = jnp.dot(hn_scr[...], win_ref[:, _OFF_KV:_OFF_ZB], preferred_element_type=F32).astype(BF16)
    zb_scr[...] = jnp.dot(hn_cur, win_ref[:, _OFF_ZB:], preferred_element_type=F32)

    key_row = lax.broadcasted_iota(jnp.int32, (3 * ATTN_BLOCK, GQA_GROUP * ATTN_BLOCK), 0)

    def scores(rows, hk):
        win = slice(rows.start, rows.start + 3 * ATTN_BLOCK)
        qs = jnp.concatenate(
            [q_scr[rows, (hk * GQA_GROUP + g) * HEAD_DIM:(hk * GQA_GROUP + g + 1) * HEAD_DIM]
             for g in range(GQA_GROUP)], axis=0)
        kw = kv_scr[win, hk * HEAD_DIM:(hk + 1) * HEAD_DIM]
        return lax.dot_general(kw, qs, _NT, preferred_element_type=F32)

    def attend(rows, hk, s):
        win = slice(rows.start, rows.start + 3 * ATTN_BLOCK)
        kpos = key_row + (t * tile + rows.start - ATTN_BLOCK)
        s = jnp.where((kpos >= 0) & (kpos < seq), s + ab_ref[hk], -jnp.inf)
        sink = sink_ref[hk]
        m = jnp.maximum(jnp.max(s, axis=0, keepdims=True), sink)
        p = jnp.exp(s - m)
        den = jnp.sum(p, axis=0, keepdims=True) + jnp.exp(sink - m)
        vw = kv_scr[win, KV_WIDTH + hk * HEAD_DIM:KV_WIDTH + (hk + 1) * HEAD_DIM]
        o = lax.dot_general(vw, p.astype(BF16), _TN, preferred_element_type=F32) / den
        for g in range(GQA_GROUP):
            c0 = (hk * GQA_GROUP + g) * HEAD_DIM
            zb = zb_scr[rows, c0:c0 + HEAD_DIM]
            y_scr[rows, GMLP_WIDTH + c0:GMLP_WIDTH + c0 + HEAD_DIM] = (
                o[:, g * ATTN_BLOCK:(g + 1) * ATTN_BLOCK].T * _silu(zb)).astype(BF16)

    pending = None
    for rows in blocks:
        for hk in range(N_KV_HEADS):
            s = scores(rows, hk)
            if pending is not None:
                attend(*pending)
            pending = (rows, hk, s)
    attend(*pending)

    o_ref[0] = x_ref[0] + jnp.dot(y_scr[...], wout_ref[...], preferred_element_type=F32)


def _even_layer(x, norm_g, w_in, ln_g, ln_b, w_s, b_s, sink, w_out):
    B, S, D = x.shape
    tile = min(EVEN_TILE, S)
    assert S % tile == 0 and tile % ATTN_BLOCK == 0
    nb = S // ATTN_BLOCK
    tpb = tile // ATTN_BLOCK

    qi = np.arange(ATTN_BLOCK)
    kj = np.arange(3 * ATTN_BLOCK) - ATTN_BLOCK
    dist = np.abs(kj[None, :] - qi[:, None]).astype(np.float32)
    slopes = np.exp2(-8.0 * np.arange(1, N_Q_HEADS + 1, dtype=np.float32) / N_Q_HEADS)
    ab = np.where(dist[None] <= WINDOW, -slopes[:, None, None] * dist[None], -np.inf).astype(np.float32)
    ab = jnp.asarray(ab.reshape(N_KV_HEADS, GQA_GROUP * ATTN_BLOCK, 3 * ATTN_BLOCK).transpose(0, 2, 1))
    sink_row = jnp.broadcast_to(sink.astype(F32).reshape(N_KV_HEADS, GQA_GROUP, 1, 1),
                                (N_KV_HEADS, GQA_GROUP, 1, ATTN_BLOCK)).reshape(N_KV_HEADS, 1, GQA_GROUP * ATTN_BLOCK)
    bs_full = jnp.repeat(b_s.astype(F32).T, GMLP_GROUP_DIM, axis=1)

    kern = functools.partial(_even_kernel, tile=tile, seq=S)
    return pl.pallas_call(
        kern,
        grid=(B, S // tile),
        in_specs=[
            pl.BlockSpec((1, tile, D), lambda b, t: (b, t, 0)),
            pl.BlockSpec((1, ATTN_BLOCK, D), lambda b, t: (b, jnp.maximum(t * tpb - 1, 0), 0)),
            pl.BlockSpec((1, ATTN_BLOCK, D), lambda b, t: (b, jnp.minimum((t + 1) * tpb, nb - 1), 0)),
            _const_spec((1, D)),
            _const_spec((D, IN_EVEN)),
            _const_spec((1, GMLP_WIDTH)),
            _const_spec((1, GMLP_WIDTH)),
            _const_spec((GMLP_GROUPS, GMLP_CHUNK, GMLP_CHUNK)),
            _const_spec((GMLP_CHUNK, GMLP_WIDTH)),
            _const_spec((N_KV_HEADS, 3 * ATTN_BLOCK, GQA_GROUP * ATTN_BLOCK)),
            _const_spec((N_KV_HEADS, 1, GQA_GROUP * ATTN_BLOCK)),
            _const_spec((GMLP_WIDTH + ATTN_WIDTH, D)),
        ],
        out_specs=pl.BlockSpec((1, tile, D), lambda b, t: (b, t, 0)),
        out_shape=jax.ShapeDtypeStruct((B, S, D), x.dtype),
        scratch_shapes=[
            pltpu.VMEM((tile + 2 * ATTN_BLOCK, D), BF16),
            pltpu.VMEM((tile, 3 * GMLP_WIDTH), F32),
            pltpu.VMEM((tile, ATTN_WIDTH), BF16),
            pltpu.VMEM((tile + 2 * ATTN_BLOCK, 2 * KV_WIDTH), BF16),
            pltpu.VMEM((tile, ATTN_WIDTH), F32),
            pltpu.VMEM((tile, GMLP_WIDTH + ATTN_WIDTH), BF16),
        ],
        compiler_params=pltpu.CompilerParams(
            dimension_semantics=("arbitrary", "arbitrary"),
            vmem_limit_bytes=V7X_VMEM_LIMIT_BYTES),
        name="even_layer",
    )(x, x, x, norm_g.reshape(1, D).astype(F32), w_in.astype(BF16),
      ln_g.reshape(1, -1).astype(F32), ln_b.reshape(1, -1).astype(F32), w_s.astype(BF16), bs_full,
      ab, sink_row, w_out.astype(BF16))


def _hgrn_lower_bound(gamma, layer):
    rows = [gamma[l:l + 1, :] for l in range(DEPTH)]
    mx = functools.reduce(jnp.maximum, rows)
    es = [jnp.exp(r - mx) for r in rows]
    tot = functools.reduce(lambda a, b: a + b, es)
    acc = jnp.zeros_like(tot)
    for l in range(1, layer + 1):
        acc = acc + es[l]
    return acc / tot


def _hgrn_scratch(tile, nseg):
    G = HGRN_HEADS // HGRN_GROUP_HEADS
    GW = HGRN_GROUP_HEADS * HGRN_HEAD_DIM
    return [
        pltpu.VMEM((tile, D_MODEL), BF16),
        pltpu.VMEM((HGRN_CHUNK, 2 * HGRN_CHUNK), BF16),
        pltpu.VMEM((nseg, G, tile, GW), F32),
        pltpu.VMEM((1, HGRN_WIDTH), F32),
        pltpu.VMEM((HGRN_HEADS, HGRN_HEAD_DIM, HGRN_HEAD_DIM), F32),
        pltpu.VMEM((G, tile, GW), F32),
        pltpu.VMEM((G, 2 * tile, GW), BF16),
        pltpu.VMEM((G, tile, GW), BF16),
        pltpu.VMEM((G, tile, GW), BF16),
        pltpu.VMEM((G, tile, GW), BF16),
        pltpu.VMEM((HGRN_HEADS, HGRN_HEAD_DIM, tile), BF16),
        pltpu.VMEM((HGRN_HEADS, HGRN_HEAD_DIM, tile), BF16),
        pltpu.VMEM((G, tile // HGRN_CHUNK * F32_SUBLANES, GW), F32),
    ]


def _hgrn_tile(x, ng_ref, w_ref, gamma_ref, write_out,
               hn_scr, scan_scr, proj_scr, lb_scr, state_scr, r_scr, gs_scr, qt_scr, qs_scr, v_scr, ktT_scr, keT_scr,
               dn_scr, *, tile, layer, reverse, nseg, seg_f, read_q, read_i, after_project=None, finish_head=None):
    C = HGRN_CHUNK
    HD = HGRN_HEAD_DIM
    W = HGRN_WIDTH
    P = C // 2
    SUB = F32_SUBLANES
    GW = HGRN_GROUP_HEADS * HD
    nchunks = tile // C

    @pl.when((pl.program_id(0) == 0) & (pl.program_id(1) == 0))
    def _():
        ti = lax.broadcasted_iota(jnp.int32, (C, C), 0)
        si = lax.broadcasted_iota(jnp.int32, (C, C), 1)
        if reverse:
            coef = (si >= ti).astype(jnp.int32) - (si >= P).astype(jnp.int32)
        else:
            coef = (si <= ti).astype(jnp.int32) - (si < P).astype(jnp.int32)
        scan = coef.astype(F32).astype(BF16)
        scan_scr[...] = jnp.concatenate([scan, scan], axis=1)

    @pl.when(pl.program_id(1) == 0)
    def _():
        state_scr[...] = jnp.zeros_like(state_scr)

    lb_scr[...] = _hgrn_lower_bound(gamma_ref[...], layer)
    hn_scr[...] = _rms_rows(x, ng_ref[...]).astype(BF16)

    near_row = C - 1 if reverse else 0
    far_row = 0 if reverse else C - 1
    t2 = lax.broadcasted_iota(jnp.int32, (C, C), 0)
    s2 = lax.broadcasted_iota(jnp.int32, (C, C), 1)
    cmask = (s2 >= t2) if reverse else (s2 <= t2)
    chunk_order = range(nchunks - 1, -1, -1) if reverse else range(nchunks)

    def piece_rows(rb, piece):
        c, r = divmod(rb * P, C)
        return slice((2 * c + piece) * C + r, (2 * c + piece) * C + r + P)

    def group_heads(g):
        return [(g * HGRN_GROUP_HEADS + hh, slice(hh * HD, (hh + 1) * HD),
                 slice(g * GW + hh * HD, g * GW + (hh + 1) * HD)) for hh in range(HGRN_GROUP_HEADS)]

    def project(g):
        for s in range(nseg):
            proj_scr[s, g] = jnp.dot(hn_scr[...], w_ref[:, s * W + g * GW:s * W + (g + 1) * GW],
                                     preferred_element_type=F32)
        if after_project is not None:
            after_project(g, slice(g * GW, (g + 1) * GW))

    def gate_and_scan(g):
        for rb in range(tile // P):
            rows = slice(rb * P, (rb + 1) * P)
            for _, lc, cols in group_heads(g):
                lb = lb_scr[:, cols]
                f = lb + (1.0 - lb) * _sigmoid(proj_scr[seg_f, g, rows, lc])
                proj_scr[seg_f, g, rows, lc] = f
                lf = jnp.log(f)
                p1 = lf.astype(BF16)
                gs_scr[g, piece_rows(rb, 0), lc] = p1
                gs_scr[g, piece_rows(rb, 1), lc] = (lf - p1.astype(F32)).astype(BF16)

        for c in range(nchunks):
            r_scr[g, c * C:(c + 1) * C] = jnp.dot(scan_scr[...], gs_scr[g, 2 * c * C:2 * (c + 1) * C],
                                                   preferred_element_type=F32)

    def decay_and_recur(g):
        heads = group_heads(g)
        for rb in range(tile // P):
            rows = slice(rb * P, (rb + 1) * P)
            c0 = (rb * P // C) * C
            for _, lc, cols in heads:
                e = jnp.exp(r_scr[g, rows, lc])
                near = slice(c0 + near_row, c0 + near_row + 1)
                far = slice(c0 + far_row, c0 + far_row + 1)
                e_mid = proj_scr[seg_f, g, near, lc] / jnp.exp(r_scr[g, near, lc])
                e_far = jnp.exp(r_scr[g, far, lc])
                qt = _silu(read_q(g, rows, lc, cols)) * e
                kt = (1.0 - proj_scr[seg_f, g, rows, lc]) / e
                qt_scr[g, rows, lc] = qt.astype(BF16)
                qs_scr[g, rows, lc] = (qt * e_mid).astype(BF16)
                gs_scr[g, piece_rows(rb, 0), lc] = kt.astype(BF16)
                gs_scr[g, piece_rows(rb, 1), lc] = (kt * e_far).astype(BF16)
                v_scr[g, rows, lc] = read_i(g, rows, lc, cols)
                if rb * P == c0:
                    dn_scr[g, c0 // C * SUB:(c0 // C + 1) * SUB, lc] = jnp.broadcast_to(e_mid * e_far, (SUB, HD))

        for c in range(nchunks):
            crows = slice(c * C, (c + 1) * C)
            for h, lc, _ in heads:
                ktT_scr[h, :, crows] = gs_scr[g, 2 * c * C:(2 * c + 1) * C, lc].T
                keT_scr[h, :, crows] = gs_scr[g, (2 * c + 1) * C:(2 * c + 2) * C, lc].T

        for c in chunk_order:
            crows = slice(c * C, (c + 1) * C)
            for h, lc, cols in heads:
                st = state_scr[h]
                v = v_scr[g, crows, lc]
                a = jnp.dot(qt_scr[g, crows, lc], ktT_scr[h, :, crows], preferred_element_type=F32)
                a = jnp.where(cmask, a, 0.0).astype(BF16)
                write_out(g, lc, cols, crows,
                          jnp.dot(jnp.concatenate([a, qs_scr[g, crows, lc]], axis=1),
                                  jnp.concatenate([v, st.astype(BF16)], axis=0), preferred_element_type=F32))
                decay = jnp.broadcast_to(dn_scr[g, c * SUB:c * SUB + 1, lc], (HD, HD)).T
                state_scr[h] = st * decay + jnp.dot(keT_scr[h, :, crows], v, preferred_element_type=F32)

        if finish_head is not None:
            for _, lc, cols in heads:
                finish_head(g, lc, cols)

    ngroups = HGRN_HEADS // HGRN_GROUP_HEADS
    for k in range(ngroups + 2):
        if k < ngroups:
            project(k)
        if 0 <= k - 1 < ngroups:
            gate_and_scan(k - 1)
        if 0 <= k - 2 < ngroups:
            decay_and_recur(k - 2)


def _odd_bwd_kernel(x_ref, ng_ref, w_ref, gamma_ref, ob_ref, qx_ref, ix_ref, *scr, tile, layer):
    proj_scr = scr[2]
    SEG_Q, SEG_F, SEG_I = 0, 1, 2

    def export_q_i(g, gcols):
        qx_ref[0, :, gcols] = proj_scr[SEG_Q, g].astype(BF16)
        ix_ref[0, :, gcols] = proj_scr[SEG_I, g].astype(BF16)

    def write_out(g, lc, cols, rows, o):
        ob_ref[0, rows, cols] = o.astype(BF16)

    _hgrn_tile(x_ref[0], ng_ref, w_ref, gamma_ref, write_out, *scr, tile=tile, layer=layer, reverse=True,
               nseg=3, seg_f=SEG_F, after_project=export_q_i,
               read_q=lambda g, rows, lc, cols: proj_scr[SEG_Q, g, rows, lc],
               read_i=lambda g, rows, lc, cols: proj_scr[SEG_I, g, rows, lc].astype(BF16))


def _odd_fwd_kernel(x_ref, ob_ref, q_ref, i_ref, ng_ref, w_ref, gamma_ref, hng_ref, wout_ref, fg_ref, o_ref,
                    of_scr, y_scr, *scr, tile, layer):
    proj_scr = scr[2]
    SEG_F, SEG_Z = 0, 1

    def write_out(g, lc, cols, rows, o):
        of_scr[g, rows, lc] = o

    def finish_head(g, lc, cols):
        o = of_scr[g, :, lc] + ob_ref[0, :, cols].astype(F32)
        o = o * lax.rsqrt(jnp.mean(o * o, axis=-1, keepdims=True) + EPS) * hng_ref[:, cols]
        y_scr[:, cols] = (o * _silu(proj_scr[SEG_Z, g, :, lc])).astype(BF16)

    _hgrn_tile(x_ref[0], ng_ref, w_ref, gamma_ref, write_out, *scr, tile=tile, layer=layer, reverse=False,
               nseg=2, seg_f=SEG_F, finish_head=finish_head,
               read_q=lambda g, rows, lc, cols: q_ref[0, rows, cols].astype(F32),
               read_i=lambda g, rows, lc, cols: i_ref[0, rows, cols])
    x2 = x_ref[0] + jnp.dot(y_scr[...], wout_ref[...], preferred_element_type=F32)
    o_ref[0] = _rms_rows(x2, fg_ref[...])


def _odd_layer_and_final_norm(x, norm_g, w_in, gamma_f, gamma_b, head_norm_g, w_out, final_g, layer):
    B, S, D = x.shape
    tile = min(ODD_TILE, S)
    assert S % tile == 0 and tile % HGRN_CHUNK == 0
    nt = S // tile
    W = HGRN_WIDTH
    w_bf = w_in.astype(BF16)
    wq, wff, wfb, wi, wz = (w_bf[:, n * W:(n + 1) * W] for n in range(5))
    w_bwd = jnp.concatenate([wq, wfb, wi], axis=1)
    w_fwd = jnp.concatenate([wff, wz], axis=1)
    ng = norm_g.reshape(1, D).astype(F32)
    params = pltpu.CompilerParams(dimension_semantics=("arbitrary", "arbitrary"),
                                  vmem_limit_bytes=V7X_VMEM_LIMIT_BYTES)

    seq_rev = pl.BlockSpec((1, tile, W), lambda b, t: (b, nt - 1 - t, 0))
    seq_fwd = pl.BlockSpec((1, tile, W), lambda b, t: (b, t, 0))
    o_b, q_proj, i_proj = pl.pallas_call(
        functools.partial(_odd_bwd_kernel, tile=tile, layer=layer),
        grid=(B, nt),
        in_specs=[
            pl.BlockSpec((1, tile, D), lambda b, t: (b, nt - 1 - t, 0)),
            _const_spec((1, D)),
            _const_spec((D, 3 * W)),
            _const_spec((DEPTH, W)),
        ],
        out_specs=[seq_rev, seq_rev, seq_rev],
        out_shape=[jax.ShapeDtypeStruct((B, S, W), BF16)] * 3,
        scratch_shapes=_hgrn_scratch(tile, 3),
        compiler_params=params,
        name="odd_layer_bwd_pass",
    )(x, ng, w_bwd, gamma_b.astype(F32))

    return pl.pallas_call(
        functools.partial(_odd_fwd_kernel, tile=tile, layer=layer),
        grid=(B, nt),
        in_specs=[
            pl.BlockSpec((1, tile, D), lambda b, t: (b, t, 0)),
            seq_fwd, seq_fwd, seq_fwd,
            _const_spec((1, D)),
            _const_spec((D, 2 * W)),
            _const_spec((DEPTH, W)),
            _const_spec((1, W)),
            _const_spec((W, D)),
            _const_spec((1, D)),
        ],
        out_specs=pl.BlockSpec((1, tile, D), lambda b, t: (b, t, 0)),
        out_shape=jax.ShapeDtypeStruct((B, S, D), x.dtype),
        scratch_shapes=[
            pltpu.VMEM((HGRN_HEADS // HGRN_GROUP_HEADS, tile, HGRN_GROUP_HEADS * HGRN_HEAD_DIM), F32),
            pltpu.VMEM((tile, W), BF16),
        ] + _hgrn_scratch(tile, 2),
        compiler_params=params,
        name="odd_layer_fwd_pass",
    )(x, o_b, q_proj, i_proj, ng, w_fwd, gamma_f.astype(F32), head_norm_g.reshape(1, W).astype(F32),
      w_out.astype(BF16), final_g.reshape(1, D).astype(F32))


def kernel(x, norm_g_even, w_in_even, gmlp_ln_g, gmlp_ln_b, gmlp_w_s, gmlp_b_s, attn_sink, w_out_even, norm_g_odd, w_in_odd, hgrn_gamma_fwd, hgrn_gamma_bwd, hgrn_head_norm_g, w_out_odd, final_norm_g):
    assert DEPTH == 2
    x = _even_layer(x, norm_g_even[0], w_in_even[0], gmlp_ln_g[0], gmlp_ln_b[0], gmlp_w_s[0], gmlp_b_s[0],
                    attn_sink[0], w_out_even[0])
    return _odd_layer_and_final_norm(x, norm_g_odd[0], w_in_odd[0], hgrn_gamma_fwd, hgrn_gamma_bwd,
                                     hgrn_head_norm_g[0], w_out_odd[0], final_norm_g, layer=1)
```

```python
import functools

import jax
import jax.numpy as jnp
import numpy as np
from jax import lax
from jax.experimental import pallas as pl
from jax.experimental.pallas import tpu as pltpu

F32 = jnp.float32
BF16 = jnp.bfloat16

D_MODEL = 1024
DEPTH = 2
MIX_WIDTH = 2 * D_MODEL
GMLP_WIDTH = MIX_WIDTH // 2
GMLP_GROUPS = 4
GMLP_GROUP_DIM = GMLP_WIDTH // GMLP_GROUPS
GMLP_CHUNK = 128
ATTN_WIDTH = MIX_WIDTH // 2
HEAD_DIM = 128
N_Q_HEADS = ATTN_WIDTH // HEAD_DIM
N_KV_HEADS = 2
GQA_GROUP = N_Q_HEADS // N_KV_HEADS
KV_WIDTH = N_KV_HEADS * HEAD_DIM
WINDOW = 128
ATTN_BLOCK = 128
HGRN_WIDTH = MIX_WIDTH
HGRN_HEAD_DIM = 128
HGRN_HEADS = HGRN_WIDTH // HGRN_HEAD_DIM
HGRN_CHUNK = 128
HGRN_GROUP_HEADS = 4
EPS = 1e-6
IN_EVEN = 3 * GMLP_WIDTH + ATTN_WIDTH + 2 * KV_WIDTH + ATTN_WIDTH

_OFF_UVZ = 0
_OFF_Q = 3 * GMLP_WIDTH
_OFF_KV = _OFF_Q + ATTN_WIDTH
_OFF_ZB = _OFF_KV + 2 * KV_WIDTH

F32_SUBLANES = 8

V7X_VMEM_LIMIT_BYTES = 58 * 1024 * 1024

EVEN_TILE = 512
ODD_TILE = 256

_NT = (((1,), (1,)), ((), ()))
_TN = (((0,), (0,)), ((), ()))


def _const_spec(shape):
    nd = len(shape)
    return pl.BlockSpec(shape, lambda *_: (0,) * nd, pipeline_mode=pl.Buffered(1))


def _rms_rows(x, g):
    return x * lax.rsqrt(jnp.mean(x * x, axis=-1, keepdims=True) + EPS) * g


def _sigmoid(z):
    return 0.5 * jnp.tanh(0.5 * z) + 0.5


def _silu(z):
    return z * _sigmoid(z)


def _even_kernel(x_ref, xp_ref, xn_ref, ng_ref, win_ref, lng_ref, lnb_ref, ws_ref, bs_ref,
                 ab_ref, sink_ref, wout_ref, o_ref,
                 hn_scr, uvz_scr, q_scr, kv_scr, zb_scr, y_scr, *, tile, seq):
    t = pl.program_id(1)
    ng = ng_ref[...]
    hn_scr[0:ATTN_BLOCK, :] = _rms_rows(xp_ref[0], ng).astype(BF16)
    hn_scr[ATTN_BLOCK:ATTN_BLOCK + tile, :] = _rms_rows(x_ref[0], ng).astype(BF16)
    hn_scr[ATTN_BLOCK + tile:, :] = _rms_rows(xn_ref[0], ng).astype(BF16)

    hn_cur = hn_scr[ATTN_BLOCK:ATTN_BLOCK + tile, :]
    blocks = [slice(j * ATTN_BLOCK, (j + 1) * ATTN_BLOCK) for j in range(tile // ATTN_BLOCK)]
    uvz_scr[...] = jnp.dot(hn_cur, win_ref[:, _OFF_UVZ:_OFF_Q], preferred_element_type=F32)
    q_scr[...] = (jnp.dot(hn_cur, win_ref[:, _OFF_Q:_OFF_KV], preferred_element_type=F32)
                  * (HEAD_DIM ** -0.5)).astype(BF16)

    lng = lng_ref[...]
    lnb = lnb_ref[...]
    for rows in blocks:
        v = uvz_scr[rows, GMLP_WIDTH:2 * GMLP_WIDTH]
        mu = jnp.mean(v, axis=-1, keepdims=True)
        vc = v - mu
        var = jnp.mean(vc * vc, axis=-1, keepdims=True)
        vn = (vc * lax.rsqrt(var + EPS) * lng + lnb).astype(BF16)
        mixed = jnp.concatenate(
            [jnp.dot(ws_ref[g], vn[:, g * GMLP_GROUP_DIM:(g + 1) * GMLP_GROUP_DIM], preferred_element_type=F32)
             for g in range(GMLP_GROUPS)], axis=1) + bs_ref[...]
        u = uvz_scr[rows, 0:GMLP_WIDTH]
        za = uvz_scr[rows, 2 * GMLP_WIDTH:3 * GMLP_WIDTH]
        y_scr[rows, 0:GMLP_WIDTH] = (u * mixed * _silu(za)).astype(BF16)

    kv_scr[...] = jnp.dot(hn_scr[...], win_ref[:, _OFF_KV:_OFF_ZB], preferred_element_type=F32).astype(BF16)
    zb_scr[...] = jnp.dot(hn_cur, win_ref[:, _OFF_ZB:], preferred_element_type=F32)

    key_row = lax.broadcasted_iota(jnp.int32, (3 * ATTN_BLOCK, GQA_GROUP * ATTN_BLOCK), 0)

    def scores(rows, hk):
        win = slice(rows.start, rows.start + 3 * ATTN_BLOCK)
        qs = jnp.concatenate(
            [q_scr[rows, (hk * GQA_GROUP + g) * HEAD_DIM:(hk * GQA_GROUP + g + 1) * HEAD_DIM]
             for g in range(GQA_GROUP)], axis=0)
        kw = kv_scr[win, hk * HEAD_DIM:(hk + 1) * HEAD_DIM]
        return lax.dot_general(kw, qs, _NT, preferred_element_type=F32)

    def attend(rows, hk, s):
        win = slice(rows.start, rows.start + 3 * ATTN_BLOCK)
        kpos = key_row + (t * tile + rows.start - ATTN_BLOCK)
        s = jnp.where((kpos >= 0) & (kpos < seq), s + ab_ref[hk], -jnp.inf)
        sink = sink_ref[hk]
        m = jnp.maximum(jnp.max(s, axis=0, keepdims=True), sink)
        p = jnp.exp(s - m)
        den = jnp.sum(p, axis=0, keepdims=True) + jnp.exp(sink - m)
        vw = kv_scr[win, KV_WIDTH + hk * HEAD_DIM:KV_WIDTH + (hk + 1) * HEAD_DIM]
        o = lax.dot_general(vw, p.astype(BF16), _TN, preferred_element_type=F32) / den
        for g in range(GQA_GROUP):
            c0 = (hk * GQA_GROUP + g) * HEAD_DIM
            zb = zb_scr[rows, c0:c0 + HEAD_DIM]
            y_scr[rows, GMLP_WIDTH + c0:GMLP_WIDTH + c0 + HEAD_DIM] = (
                o[:, g * ATTN_BLOCK:(g + 1) * ATTN_BLOCK].T * _silu(zb)).astype(BF16)

    pending = None
    for rows in blocks:
        for hk in range(N_KV_HEADS):
            s = scores(rows, hk)
            if pending is not None:
                attend(*pending)
            pending = (rows, hk, s)
    attend(*pending)

    o_ref[0] = x_ref[0] + jnp.dot(y_scr[...], wout_ref[...], preferred_element_type=F32)


def _even_layer(x, norm_g, w_in, ln_g, ln_b, w_s, b_s, sink, w_out):
    B, S, D = x.shape
    tile = min(EVEN_TILE, S)
    assert S % tile == 0 and tile % ATTN_BLOCK == 0
    nb = S // ATTN_BLOCK
    tpb = tile // ATTN_BLOCK

    qi = np.arange(ATTN_BLOCK)
    kj = np.arange(3 * ATTN_BLOCK) - ATTN_BLOCK
    dist = np.abs(kj[None, :] - qi[:, None]).astype(np.float32)
    slopes = np.exp2(-8.0 * np.arange(1, N_Q_HEADS + 1, dtype=np.float32) / N_Q_HEADS)
    ab = np.where(dist[None] <= WINDOW, -slopes[:, None, None] * dist[None], -np.inf).astype(np.float32)
    ab = jnp.asarray(ab.reshape(N_KV_HEADS, GQA_GROUP * ATTN_BLOCK, 3 * ATTN_BLOCK).transpose(0, 2, 1))
    sink_row = jnp.broadcast_to(sink.astype(F32).reshape(N_KV_HEADS, GQA_GROUP, 1, 1),
                                (N_KV_HEADS, GQA_GROUP, 1, ATTN_BLOCK)).reshape(N_KV_HEADS, 1, GQA_GROUP * ATTN_BLOCK)
    bs_full = jnp.repeat(b_s.astype(F32).T, GMLP_GROUP_DIM, axis=1)

    kern = functools.partial(_even_kernel, tile=tile, seq=S)
    return pl.pallas_call(
        kern,
        grid=(B, S // tile),
        in_specs=[
            pl.BlockSpec((1, tile, D), lambda b, t: (b, t, 0)),
            pl.BlockSpec((1, ATTN_BLOCK, D), lambda b, t: (b, jnp.maximum(t * tpb - 1, 0), 0)),
            pl.BlockSpec((1, ATTN_BLOCK, D), lambda b, t: (b, jnp.minimum((t + 1) * tpb, nb - 1), 0)),
            _const_spec((1, D)),
            _const_spec((D, IN_EVEN)),
            _const_spec((1, GMLP_WIDTH)),
            _const_spec((1, GMLP_WIDTH)),
            _const_spec((GMLP_GROUPS, GMLP_CHUNK, GMLP_CHUNK)),
            _const_spec((GMLP_CHUNK, GMLP_WIDTH)),
            _const_spec((N_KV_HEADS, 3 * ATTN_BLOCK, GQA_GROUP * ATTN_BLOCK)),
            _const_spec((N_KV_HEADS, 1, GQA_GROUP * ATTN_BLOCK)),
            _const_spec((GMLP_WIDTH + ATTN_WIDTH, D)),
        ],
        out_specs=pl.BlockSpec((1, tile, D), lambda b, t: (b, t, 0)),
        out_shape=jax.ShapeDtypeStruct((B, S, D), x.dtype),
        scratch_shapes=[
            pltpu.VMEM((tile + 2 * ATTN_BLOCK, D), BF16),
            pltpu.VMEM((tile, 3 * GMLP_WIDTH), F32),
            pltpu.VMEM((tile, ATTN_WIDTH), BF16),
            pltpu.VMEM((tile + 2 * ATTN_BLOCK, 2 * KV_WIDTH), BF16),
            pltpu.VMEM((tile, ATTN_WIDTH), F32),
            pltpu.VMEM((tile, GMLP_WIDTH + ATTN_WIDTH), BF16),
        ],
        compiler_params=pltpu.CompilerParams(
            dimension_semantics=("arbitrary", "arbitrary"),
            vmem_limit_bytes=V7X_VMEM_LIMIT_BYTES),
        name="even_layer",
    )(x, x, x, norm_g.reshape(1, D).astype(F32), w_in.astype(BF16),
      ln_g.reshape(1, -1).astype(F32), ln_b.reshape(1, -1).astype(F32), w_s.astype(BF16), bs_full,
      ab, sink_row, w_out.astype(BF16))


def _hgrn_lower_bound(gamma, layer):
    rows = [gamma[l:l + 1, :] for l in range(DEPTH)]
    mx = functools.reduce(jnp.maximum, rows)
    es = [jnp.exp(r - mx) for r in rows]
    tot = functools.reduce(lambda a, b: a + b, es)
    acc = jnp.zeros_like(tot)
    for l in range(1, layer + 1):
        acc = acc + es[l]
    return acc / tot


def _hgrn_scratch(tile, nseg):
    G = HGRN_HEADS // HGRN_GROUP_HEADS
    GW = HGRN_GROUP_HEADS * HGRN_HEAD_DIM
    return [
        pltpu.VMEM((tile, D_MODEL), BF16),
        pltpu.VMEM((HGRN_CHUNK, 2 * HGRN_CHUNK), BF16),
        pltpu.VMEM((nseg, G, tile, GW), F32),
        pltpu.VMEM((1, HGRN_WIDTH), F32),
        pltpu.VMEM((HGRN_HEADS, HGRN_HEAD_DIM, HGRN_HEAD_DIM), F32),
        pltpu.VMEM((G, tile, GW), F32),
        pltpu.VMEM((G, 2 * tile, GW), BF16),
        pltpu.VMEM((G, tile, GW), BF16),
        pltpu.VMEM((G, tile, GW), BF16),
        pltpu.VMEM((G, tile, GW), BF16),
        pltpu.VMEM((HGRN_HEADS, HGRN_HEAD_DIM, tile), BF16),
        pltpu.VMEM((HGRN_HEADS, HGRN_HEAD_DIM, tile), BF16),
        pltpu.VMEM((G, tile // HGRN_CHUNK * F32_SUBLANES, GW), F32),
    ]


def _hgrn_tile(x, ng_ref, w_ref, gamma_ref, write_out,
               hn_scr, scan_scr, proj_scr, lb_scr, state_scr, r_scr, gs_scr, qt_scr, qs_scr, v_scr, ktT_scr, keT_scr,
               dn_scr, *, tile, layer, reverse, nseg, seg_f, read_q, read_i, after_project=None, finish_head=None):
    C = HGRN_CHUNK
    HD = HGRN_HEAD_DIM
    W = HGRN_WIDTH
    P = C // 2
    SUB = F32_SUBLANES
    GW = HGRN_GROUP_HEADS * HD
    nchunks = tile // C

    @pl.when((pl.program_id(0) == 0) & (pl.program_id(1) == 0))
    def _():
        ti = lax.broadcasted_iota(jnp.int32, (C, C), 0)
        si = lax.broadcasted_iota(jnp.int32, (C, C), 1)
        if reverse:
            coef = (si >= ti).astype(jnp.int32) - (si >= P).astype(jnp.int32)
        else:
            coef = (si <= ti).astype(jnp.int32) - (si < P).astype(jnp.int32)
        scan = coef.astype(F32).astype(BF16)
        scan_scr[...] = jnp.concatenate([scan, scan], axis=1)

    @pl.when(pl.program_id(1) == 0)
    def _():
        state_scr[...] = jnp.zeros_like(state_scr)

    lb_scr[...] = _hgrn_lower_bound(gamma_ref[...], layer)
    hn_scr[...] = _rms_rows(x, ng_ref[...]).astype(BF16)

    near_row = C - 1 if reverse else 0
    far_row = 0 if reverse else C - 1
    t2 = lax.broadcasted_iota(jnp.int32, (C, C), 0)
    s2 = lax.broadcasted_iota(jnp.int32, (C, C), 1)
    cmask = (s2 >= t2) if reverse else (s2 <= t2)
    chunk_order = range(nchunks - 1, -1, -1) if reverse else range(nchunks)

    def piece_rows(rb, piece):
        c, r = divmod(rb * P, C)
        return slice((2 * c + piece) * C + r, (2 * c + piece) * C + r + P)

    def group_heads(g):
        return [(g * HGRN_GROUP_HEADS + hh, slice(hh * HD, (hh + 1) * HD),
                 slice(g * GW + hh * HD, g * GW + (hh + 1) * HD)) for hh in range(HGRN_GROUP_HEADS)]

    def project(g):
        for s in range(nseg):
            proj_scr[s, g] = jnp.dot(hn_scr[...], w_ref[:, s * W + g * GW:s * W + (g + 1) * GW],
                                     preferred_element_type=F32)
        if after_project is not None:
            after_project(g, slice(g * GW, (g + 1) * GW))

    def gate_and_scan(g):
        for rb in range(tile // P):
            rows = slice(rb * P, (rb + 1) * P)
            for _, lc, cols in group_heads(g):
                lb = lb_scr[:, cols]
                f = lb + (1.0 - lb) * _sigmoid(proj_scr[seg_f, g, rows, lc])
                proj_scr[seg_f, g, rows, lc] = f
                lf = jnp.log(f)
                p1 = lf.astype(BF16)
                gs_scr[g, piece_rows(rb, 0), lc] = p1
                gs_scr[g, piece_rows(rb, 1), lc] = (lf - p1.astype(F32)).astype(BF16)

        for c in range(nchunks):
            r_scr[g, c * C:(c + 1) * C] = jnp.dot(scan_scr[...], gs_scr[g, 2 * c * C:2 * (c + 1) * C],
                                                   preferred_element_type=F32)

    def decay_and_recur(g):
        heads = group_heads(g)
        for rb in range(tile // P):
            rows = slice(rb * P, (rb + 1) * P)
            c0 = (rb * P // C) * C
            for _, lc, cols in heads:
                e = jnp.exp(r_scr[g, rows, lc])
                near = slice(c0 + near_row, c0 + near_row + 1)
                far = slice(c0 + far_row, c0 + far_row + 1)
                e_mid = proj_scr[seg_f, g, near, lc] / jnp.exp(r_scr[g, near, lc])
                e_far = jnp.exp(r_scr[g, far, lc])
                qt = _silu(read_q(g, rows, lc, cols)) * e
                kt = (1.0 - proj_scr[seg_f, g, rows, lc]) / e
                qt_scr[g, rows, lc] = qt.astype(BF16)
                qs_scr[g, rows, lc] = (qt * e_mid).astype(BF16)
                gs_scr[g, piece_rows(rb, 0), lc] = kt.astype(BF16)
                gs_scr[g, piece_rows(rb, 1), lc] = (kt * e_far).astype(BF16)
                v_scr[g, rows, lc] = read_i(g, rows, lc, cols)
                if rb * P == c0:
                    dn_scr[g, c0 // C * SUB:(c0 // C + 1) * SUB, lc] = jnp.broadcast_to(e_mid * e_far, (SUB, HD))

        for c in range(nchunks):
            crows = slice(c * C, (c + 1) * C)
            for h, lc, _ in heads:
                ktT_scr[h, :, crows] = gs_scr[g, 2 * c * C:(2 * c + 1) * C, lc].T
                keT_scr[h, :, crows] = gs_scr[g, (2 * c + 1) * C:(2 * c + 2) * C, lc].T

        for c in chunk_order:
            crows = slice(c * C, (c + 1) * C)
            for h, lc, cols in heads:
                st = state_scr[h]
                v = v_scr[g, crows, lc]
                a = jnp.dot(qt_scr[g, crows, lc], ktT_scr[h, :, crows], preferred_element_type=F32)
                a = jnp.where(cmask, a, 0.0).astype(BF16)
                write_out(g, lc, cols, crows,
                          jnp.dot(jnp.concatenate([a, qs_scr[g, crows, lc]], axis=1),
                                  jnp.concatenate([v, st.astype(BF16)], axis=0), preferred_element_type=F32))
                decay = jnp.broadcast_to(dn_scr[g, c * SUB:c * SUB + 1, lc], (HD, HD)).T
                state_scr[h] = st * decay + jnp.dot(keT_scr[h, :, crows], v, preferred_element_type=F32)

        if finish_head is not None:
            for _, lc, cols in heads:
                finish_head(g, lc, cols)

    ngroups = HGRN_HEADS // HGRN_GROUP_HEADS
    for k in range(ngroups + 2):
        if k < ngroups:
            project(k)
        if 0 <= k - 1 < ngroups:
            gate_and_scan(k - 1)
        if 0 <= k - 2 < ngroups:
            decay_and_recur(k - 2)


def _odd_bwd_kernel(x_ref, ng_ref, w_ref, gamma_ref, ob_ref, qx_ref, ix_ref, *scr, tile, layer):
    proj_scr = scr[2]
    SEG_Q, SEG_F, SEG_I = 0, 1, 2

    def export_q_i(g, gcols):
        qx_ref[0, :, gcols] = proj_scr[SEG_Q, g]
        ix_ref[0, :, gcols] = proj_scr[SEG_I, g].astype(BF16)

    def write_out(g, lc, cols, rows, o):
        ob_ref[0, rows, cols] = o

    _hgrn_tile(x_ref[0], ng_ref, w_ref, gamma_ref, write_out, *scr, tile=tile, layer=layer, reverse=True,
               nseg=3, seg_f=SEG_F, after_project=export_q_i,
               read_q=lambda g, rows, lc, cols: proj_scr[SEG_Q, g, rows, lc],
               read_i=lambda g, rows, lc, cols: proj_scr[SEG_I, g, rows, lc].astype(BF16))


def _odd_fwd_kernel(x_ref, ob_ref, q_ref, i_ref, ng_ref, w_ref, gamma_ref, hng_ref, wout_ref, fg_ref, o_ref,
                    of_scr, y_scr, *scr, tile, layer):
    proj_scr = scr[2]
    SEG_F, SEG_Z = 0, 1

    def write_out(g, lc, cols, rows, o):
        of_scr[g, rows, lc] = o

    def finish_head(g, lc, cols):
        o = of_scr[g, :, lc] + ob_ref[0, :, cols]
        o = o * lax.rsqrt(jnp.mean(o * o, axis=-1, keepdims=True) + EPS) * hng_ref[:, cols]
        y_scr[:, cols] = (o * _silu(proj_scr[SEG_Z, g, :, lc])).astype(BF16)
        if lc.stop == GROUP_WIDTH:
            gcols = slice(g * GROUP_WIDTH, (g + 1) * GROUP_WIDTH)
            x2[0] = x2[0] + jnp.dot(y_scr[:, gcols], wout_ref[gcols, :], preferred_element_type=F32)

    GROUP_WIDTH = HGRN_GROUP_HEADS * HGRN_HEAD_DIM
    x2 = [x_ref[0]]
    _hgrn_tile(x_ref[0], ng_ref, w_ref, gamma_ref, write_out, *scr, tile=tile, layer=layer, reverse=False,
               nseg=2, seg_f=SEG_F, finish_head=finish_head,
               read_q=lambda g, rows, lc, cols: q_ref[0, rows, cols],
               read_i=lambda g, rows, lc, cols: i_ref[0, rows, cols])
    o_ref[0] = _rms_rows(x2[0], fg_ref[...])


def _odd_layer_and_final_norm(x, norm_g, w_in, gamma_f, gamma_b, head_norm_g, w_out, final_g, layer):
    B, S, D = x.shape
    tile = min(ODD_TILE, S)
    assert S % tile == 0 and tile % HGRN_CHUNK == 0
    nt = S // tile
    W = HGRN_WIDTH
    w_bf = w_in.astype(BF16)
    wq, wff, wfb, wi, wz = (w_bf[:, n * W:(n + 1) * W] for n in range(5))
    w_bwd = jnp.concatenate([wq, wfb, wi], axis=1)
    w_fwd = jnp.concatenate([wff, wz], axis=1)
    ng = norm_g.reshape(1, D).astype(F32)
    params = pltpu.CompilerParams(dimension_semantics=("arbitrary", "arbitrary"),
                                  vmem_limit_bytes=V7X_VMEM_LIMIT_BYTES)

    seq_rev = pl.BlockSpec((1, tile, W), lambda b, t: (b, nt - 1 - t, 0))
    seq_fwd = pl.BlockSpec((1, tile, W), lambda b, t: (b, t, 0))
    o_b, q_proj, i_proj = pl.pallas_call(
        functools.partial(_odd_bwd_kernel, tile=tile, layer=layer),
        grid=(B, nt),
        in_specs=[
            pl.BlockSpec((1, tile, D), lambda b, t: (b, nt - 1 - t, 0)),
            _const_spec((1, D)),
            _const_spec((D, 3 * W)),
            _const_spec((DEPTH, W)),
        ],
        out_specs=[seq_rev, seq_rev, seq_rev],
        out_shape=[jax.ShapeDtypeStruct((B, S, W), F32), jax.ShapeDtypeStruct((B, S, W), F32),
                   jax.ShapeDtypeStruct((B, S, W), BF16)],
        scratch_shapes=_hgrn_scratch(tile, 3),
        compiler_params=params,
        name="odd_layer_bwd_pass",
    )(x, ng, w_bwd, gamma_b.astype(F32))

    return pl.pallas_call(
        functools.partial(_odd_fwd_kernel, tile=tile, layer=layer),
        grid=(B, nt),
        in_specs=[
            pl.BlockSpec((1, tile, D), lambda b, t: (b, t, 0)),
            seq_fwd, seq_fwd, seq_fwd,
            _const_spec((1, D)),
            _const_spec((D, 2 * W)),
            _const_spec((DEPTH, W)),
            _const_spec((1, W)),
            _const_spec((W, D)),
            _const_spec((1, D)),
        ],
        out_specs=pl.BlockSpec((1, tile, D), lambda b, t: (b, t, 0)),
        out_shape=jax.ShapeDtypeStruct((B, S, D), x.dtype),
        scratch_shapes=[
            pltpu.VMEM((HGRN_HEADS // HGRN_GROUP_HEADS, tile, HGRN_GROUP_HEADS * HGRN_HEAD_DIM), F32),
            pltpu.VMEM((tile, W), BF16),
        ] + _hgrn_scratch(tile, 2),
        compiler_params=params,
        name="odd_layer_fwd_pass",
    )(x, o_b, q_proj, i_proj, ng, w_fwd, gamma_f.astype(F32), head_norm_g.reshape(1, W).astype(F32),
      w_out.astype(BF16), final_g.reshape(1, D).astype(F32))


def kernel(x, norm_g_even, w_in_even, gmlp_ln_g, gmlp_ln_b, gmlp_w_s, gmlp_b_s, attn_sink, w_out_even, norm_g_odd, w_in_odd, hgrn_gamma_fwd, hgrn_gamma_bwd, hgrn_head_norm_g, w_out_odd, final_norm_g):
    assert DEPTH == 2
    x = _even_layer(x, norm_g_even[0], w_in_even[0], gmlp_ln_g[0], gmlp_ln_b[0], gmlp_w_s[0], gmlp_b_s[0],
                    attn_sink[0], w_out_even[0])
    return _odd_layer_and_final_norm(x, norm_g_odd[0], w_in_odd[0], hgrn_gamma_fwd, hgrn_gamma_bwd,
                                     hgrn_head_norm_g[0], w_out_odd[0], final_norm_g, layer=1)
```
